```python
import jax, jax.numpy as jnp
from jax import lax
import numpy as np

D_MODEL = 1024
BATCH = 4
SEQ = 8192
DEPTH = 2
DEC_BATCH = 32
DEC_SEQ = 8
PAST_LEN = 16384
PAGE_SIZE = 128

N_MIXERS = 2
N_HEADS = 16
HEAD_DIM = D_MODEL // N_HEADS
D_FF = ((8 * D_MODEL // 3 + 127) // 128) * 128
CONV_W = 3
BLOCK_Q = 128
NORM_EPS = 1e-6
N_ATTN = (DEPTH + 1) // 2
N_CONV = DEPTH // 2

kernel_name = 'fox_shortconv_convffn_hybrid_step'


def rmsnorm(x, g):
    xf = x.astype(jnp.float32)
    y = xf * lax.rsqrt(jnp.mean(xf * xf, axis=-1, keepdims=True) + NORM_EPS)
    return (y * g.astype(jnp.float32)).astype(x.dtype)


def causal_dwconv(x, prev, w):
    t_len = x.shape[1]
    xp = jnp.concatenate([prev.astype(x.dtype), x], axis=1)
    y = w[0] * xp[:, 0:t_len]
    for j in range(1, CONV_W):
        y = y + w[j] * xp[:, j:j + t_len]
    return y, xp[:, -(CONV_W - 1):]


def short_conv_mixer(h, prev, w_in, w_dw, w_out):
    b_gate, c_gate, u = jnp.split(h @ w_in, 3, axis=-1)
    z, new_prev = causal_dwconv(c_gate * u, prev, w_dw)
    return (b_gate * z) @ w_out, new_prev


def conv_ffn(h, prev, w_a, w_b, w_dw, w_down):
    a = h @ w_a
    a_conv, new_prev = causal_dwconv(a, prev, w_dw)
    return (jax.nn.silu(a_conv) * (h @ w_b)) @ w_down, new_prev


def fox_project(h, w_qkv, w_f, b_f):
    b, t, _ = h.shape
    q, k, v = jnp.split(h @ w_qkv, 3, axis=-1)
    shp = (b, t, N_HEADS, HEAD_DIM)
    logf = jax.nn.log_sigmoid((h @ w_f + b_f).astype(jnp.float32))
    return q.reshape(shp), k.reshape(shp), v.reshape(shp), logf


def fox_prompt(q, k, v, logf):
    b, s_len, h, d = q.shape
    scale = HEAD_DIM ** -0.5
    c = jnp.cumsum(logf, axis=1)
    c_t = jnp.transpose(c, (0, 2, 1))
    k_pos = jnp.arange(s_len)

    def one_block(i):
        start = i * BLOCK_Q
        qb = lax.dynamic_slice_in_dim(q, start, BLOCK_Q, axis=1)
        cq = lax.dynamic_slice_in_dim(c_t, start, BLOCK_Q, axis=2)
        s = jnp.einsum('bqhd,bkhd->bhqk', qb, k, preferred_element_type=jnp.float32) * scale
        s = s + cq[:, :, :, None] - c_t[:, :, None, :]
        q_pos = start + jnp.arange(BLOCK_Q)
        mask = k_pos[None, :] <= q_pos[:, None]
        s = jnp.where(mask[None, None], s, -jnp.inf)
        p = jax.nn.softmax(s, axis=-1)
        return jnp.einsum('bhqk,bkhd->bqhd', p.astype(v.dtype), v)

    out = lax.map(one_block, jnp.arange(s_len // BLOCK_Q))
    return jnp.moveaxis(out, 0, 1).reshape(b, s_len, h, d)


def fox_sample(q, k, v, logf, ck, cv, clogf, page_table):
    scale = HEAD_DIM ** -0.5
    db = q.shape[0]
    logf_past = clogf[page_table].astype(jnp.float32).reshape(db, -1, N_HEADS)
    suf = lax.cumsum(logf_past, axis=1, reverse=True) - logf_past
    c_new = jnp.cumsum(logf, axis=1)

    def one_seq(args):
        qb, kb, vb, sufb, cnb, pages = args
        kp = ck[pages].reshape(-1, N_HEADS, HEAD_DIM)
        vp = cv[pages].reshape(-1, N_HEADS, HEAD_DIM)
        cn = cnb.T
        s_past = jnp.einsum('qhd,khd->hqk', qb, kp, preferred_element_type=jnp.float32) * scale
        s_past = s_past + cn[:, :, None] + sufb.T[:, None, :]
        s_new = jnp.einsum('qhd,khd->hqk', qb, kb, preferred_element_type=jnp.float32) * scale
        s_new = s_new + cn[:, :, None] - cn[:, None, :]
        t_len = qb.shape[0]
        causal = jnp.arange(t_len)[None, :] <= jnp.arange(t_len)[:, None]
        s_new = jnp.where(causal[None], s_new, -jnp.inf)
        p = jax.nn.softmax(jnp.concatenate([s_past, s_new], axis=-1), axis=-1)
        n_past = kp.shape[0]
        o = jnp.einsum('hqk,khd->qhd', p[..., :n_past].astype(vp.dtype), vp)
        return o + jnp.einsum('hqk,khd->qhd', p[..., n_past:].astype(vb.dtype), vb)

    return lax.map(one_seq, (q, k, v, suf, c_new, page_table))


def setup_inputs(seed: int = 0) -> dict:
    key = jax.random.key(seed)
    ks = jax.random.split(key, 24)
    f32 = jnp.float32

    def nrm(k, shape, scale):
        return jax.random.normal(k, shape, f32) * scale

    n_pages = PAST_LEN // PAGE_SIZE
    n_pool = (5 * DEC_BATCH * n_pages) // 4
    d = D_MODEL
    x_prompt = nrm(ks[0], (BATCH, SEQ, d), 1.0)
    x_sample = nrm(ks[1], (DEC_BATCH, DEC_SEQ, d), 1.0)
    attn_b_f = jnp.linspace(1.0, 7.0, N_HEADS, dtype=f32)[None, :] + nrm(ks[2], (N_ATTN, N_HEADS), 0.01)
    cache_k = nrm(ks[3], (N_ATTN, n_pool, PAGE_SIZE, N_HEADS, HEAD_DIM), 1.0)
    cache_v = nrm(ks[4], (N_ATTN, n_pool, PAGE_SIZE, N_HEADS, HEAD_DIM), 1.0)
    cache_logf = jax.nn.log_sigmoid(attn_b_f[:, None, None, :] + nrm(ks[5], (N_ATTN, n_pool, PAGE_SIZE, N_HEADS), 0.5))
    state_conv = nrm(ks[6], (N_CONV, DEC_BATCH, CONV_W - 1, d), 1.0)
    state_ffn = nrm(ks[7], (DEPTH, DEC_BATCH, CONV_W - 1, D_FF), 1.0)
    page_table = jax.random.permutation(ks[8], n_pool)[:DEC_BATCH * n_pages].reshape(DEC_BATCH, n_pages).astype(jnp.int32)
    norm_mix = 1.0 + nrm(ks[9], (DEPTH, d), 0.02)
    norm_ffn = 1.0 + nrm(ks[10], (DEPTH, d), 0.02)
    norm_final = 1.0 + nrm(ks[11], (d,), 0.02)
    attn_w_qkv = nrm(ks[12], (N_ATTN, d, 3 * d), d ** -0.5)
    attn_w_f = nrm(ks[13], (N_ATTN, d, N_HEADS), 0.5 * d ** -0.5)
    attn_w_o = nrm(ks[14], (N_ATTN, d, d), d ** -0.5)
    conv_w_in = nrm(ks[15], (N_CONV, d, 3 * d), d ** -0.5)
    conv_w_dw = nrm(ks[16], (N_CONV, CONV_W, d), CONV_W ** -0.5)
    conv_w_out = nrm(ks[17], (N_CONV, d, d), d ** -0.5)
    ffn_w_a = nrm(ks[18], (DEPTH, d, D_FF), d ** -0.5)
    ffn_w_b = nrm(ks[19], (DEPTH, d, D_FF), d ** -0.5)
    ffn_w_dw = nrm(ks[20], (DEPTH, CONV_W, D_FF), CONV_W ** -0.5)
    ffn_w_down = nrm(ks[21], (DEPTH, D_FF, d), D_FF ** -0.5)
    return {'x_prompt': x_prompt, 'x_sample': x_sample,
            'cache_k': cache_k, 'cache_v': cache_v, 'cache_logf': cache_logf,
            'state_conv': state_conv, 'state_ffn': state_ffn, 'page_table': page_table,
            'norm_mix': norm_mix, 'norm_ffn': norm_ffn, 'norm_final': norm_final,
            'attn_w_qkv': attn_w_qkv, 'attn_w_f': attn_w_f, 'attn_b_f': attn_b_f, 'attn_w_o': attn_w_o,
            'conv_w_in': conv_w_in, 'conv_w_dw': conv_w_dw, 'conv_w_out': conv_w_out,
            'ffn_w_a': ffn_w_a, 'ffn_w_b': ffn_w_b, 'ffn_w_dw': ffn_w_dw, 'ffn_w_down': ffn_w_down}


def reference(x_prompt, x_sample, cache_k, cache_v, cache_logf, state_conv, state_ffn, page_table,
              norm_mix, norm_ffn, norm_final, attn_w_qkv, attn_w_f, attn_b_f, attn_w_o,
              conv_w_in, conv_w_dw, conv_w_out, ffn_w_a, ffn_w_b, ffn_w_dw, ffn_w_down):

    def run(x, paged, conv_prev, ffn_prev):
        ks, vs, lfs, convs, ffns = [], [], [], [], []
        for layer in range(DEPTH):
            h = rmsnorm(x, norm_mix[layer])
            if layer % N_MIXERS == 0:
                a = layer // N_MIXERS
                q, k, v, logf = fox_project(h, attn_w_qkv[a], attn_w_f[a], attn_b_f[a])
                if paged:
                    o = fox_sample(q, k, v, logf, cache_k[a], cache_v[a], cache_logf[a], page_table)
                else:
                    o = fox_prompt(q, k, v, logf)
                mix = o.reshape(o.shape[0], o.shape[1], D_MODEL).astype(x.dtype) @ attn_w_o[a]
                ks.append(k)
                vs.append(v)
                lfs.append(logf)
            else:
                c = layer // N_MIXERS
                mix, st = short_conv_mixer(h, conv_prev[c], conv_w_in[c], conv_w_dw[c], conv_w_out[c])
                convs.append(st)
            x = x + mix
            h = rmsnorm(x, norm_ffn[layer])
            f, st = conv_ffn(h, ffn_prev[layer], ffn_w_a[layer], ffn_w_b[layer], ffn_w_dw[layer], ffn_w_down[layer])
            ffns.append(st)
            x = x + f
        return (rmsnorm(x, norm_final), jnp.stack(ks), jnp.stack(vs), jnp.stack(lfs),
                jnp.stack(convs), jnp.stack(ffns))

    b = x_prompt.shape[0]
    conv_zero = jnp.zeros((N_CONV, b, CONV_W - 1, D_MODEL), x_prompt.dtype)
    ffn_zero = jnp.zeros((DEPTH, b, CONV_W - 1, D_FF), x_prompt.dtype)
    y_prompt, k_prompt, v_prompt, logf_prompt, conv_prompt, ffn_prompt = run(x_prompt, False, conv_zero, ffn_zero)
    y_sample, k_sample, v_sample, logf_sample, conv_sample, ffn_sample = run(x_sample, True, state_conv, state_ffn)
    return (y_prompt, y_sample, k_prompt, v_prompt, logf_prompt, conv_prompt, ffn_prompt,
            k_sample, v_sample, logf_sample, conv_sample, ffn_sample)
```

```python
import functools

import numpy as np
import jax
import jax.numpy as jnp
from jax import lax
from jax.experimental import pallas as pl
from jax.experimental.pallas import tpu as pltpu

F32 = jnp.float32
BF16 = jnp.bfloat16
NORM_EPS = 1e-6
CONV_W = 3
LANES = 128
V7X_VMEM_LIMIT = 56 * 1024 * 1024
NT_DIMS = (((1,), (1,)), ((), ()))


def _params(n_axes):
    return pltpu.CompilerParams(dimension_semantics=("arbitrary",) * n_axes,
                                vmem_limit_bytes=V7X_VMEM_LIMIT)


def _rmsnorm(x, g):
    ms = jnp.mean(x * x, axis=-1, keepdims=True)
    return x * lax.rsqrt(ms + NORM_EPS) * g


def _log_sigmoid(z):
    return jnp.minimum(z, 0.0) - jnp.log1p(jnp.exp(-jnp.abs(z)))


def _split3(x):
    hi = x.astype(BF16)
    r1 = x - hi.astype(F32)
    mid = r1.astype(BF16)
    lo = (r1 - mid.astype(F32)).astype(BF16)
    return hi, mid, lo


def _dot3(sel, x, *, nt=False, sel_right=False):
    out = None
    for part in _split3(x):
        if sel_right:
            t = jnp.dot(part, sel, preferred_element_type=F32)
        elif nt:
            t = lax.dot_general(sel, part, NT_DIMS, preferred_element_type=F32)
        else:
            t = jnp.dot(sel, part, preferred_element_type=F32)
        out = t if out is None else out + t
    return out


def _proj_kernel(x_ref, g_ref, wqkv_ref, wf_ref, bf_ref, tri_ref, *rest,
                 tiles_per_seq, n_heads, head_dim, ext):
    if ext:
        selq_ref, selk_ref, k_ref, v_ref, lf_ref, qe_ref, ke_ref, vb_ref, carry_ref = rest
    else:
        k_ref, v_ref, lf_ref, c_ref, qs_ref, carry_ref = rest
    tm, d = x_ref.shape
    i = pl.program_id(0)

    @pl.when(i % tiles_per_seq == 0)
    def _():
        carry_ref[...] = jnp.zeros_like(carry_ref)

    h = _rmsnorm(x_ref[...], g_ref[...]).astype(BF16)
    qkv = jnp.dot(h, wqkv_ref[...], preferred_element_type=F32)
    q = qkv[:, :d] * (head_dim ** -0.5)
    k = qkv[:, d:2 * d]
    v = qkv[:, 2 * d:]
    k_ref[...] = k
    v_ref[...] = v

    z = jnp.dot(h, wf_ref[...], preferred_element_type=F32) + bf_ref[...]
    lane = lax.broadcasted_iota(jnp.int32, z.shape, 1)
    lf = jnp.where(lane < n_heads, _log_sigmoid(z), 0.0)
    lf_ref[...] = lf

    c = _dot3(tri_ref[...], lf) + carry_ref[...]
    carry_ref[...] = c[tm - 1:tm, :]

    if not ext:
        c_ref[...] = c
        qs_ref[...] = q
        return

    c_hi, c_mid, c_lo = (p.astype(F32) for p in _split3(c))
    ones = (lane == 3 * n_heads).astype(F32)
    c3 = (c_hi + pltpu.roll(c_mid, n_heads, 1) + pltpu.roll(c_lo, 2 * n_heads, 1) + ones).astype(BF16)
    bq = jnp.dot(c3, selq_ref[...], preferred_element_type=F32)
    bk = jnp.dot(c3, selk_ref[...], preferred_element_type=F32)
    lane2 = lax.broadcasted_iota(jnp.int32, (1, bq.shape[1]), 1)
    own = (((lane2 // head_dim) + 1) & 2) == 0

    def dup(a):
        return jnp.concatenate([a[:, (j // 2) * LANES:(j // 2 + 1) * LANES] for j in range(n_heads)], axis=1)

    qe_ref[...] = jnp.where(own, dup(q), bq).astype(BF16)
    ke_ref[...] = jnp.where(own, dup(k), bk).astype(BF16)
    vb_ref[...] = v.astype(BF16)


def _bias_selectors(n_heads, head_dim):
    selq = np.zeros((LANES, n_heads * LANES), np.float32)
    selk = np.zeros((LANES, n_heads * LANES), np.float32)
    one = 3 * n_heads
    for h in range(n_heads):
        base = h * LANES + (head_dim if h % 2 == 0 else 0)
        for part in range(3):
            selq[part * n_heads + h, base + 2 * part] = 1.0
            selq[one, base + 2 * part + 1] = 1.0
            selk[one, base + 2 * part] = 1.0
            selk[part * n_heads + h, base + 2 * part + 1] = -1.0
    return jnp.asarray(selq, BF16), jnp.asarray(selk, BF16)


def _proj(x2, g, wqkv, wf, bf, *, seq_len, n_heads, head_dim, tm, ext):
    r, d = x2.shape
    assert 2 * head_dim == LANES and n_heads % 2 == 0 and 3 * n_heads < LANES
    assert r % tm == 0
    if seq_len >= tm:
        assert seq_len % tm == 0
        tiles_per_seq = seq_len // tm
        tri = np.tril(np.ones((tm, tm), np.float32))
    else:
        assert tm % seq_len == 0
        tiles_per_seq = 1
        idx = np.arange(tm)
        tri = ((idx[:, None] >= idx[None, :]) & (idx[:, None] // seq_len == idx[None, :] // seq_len))
        tri = tri.astype(np.float32)
    tri = jnp.asarray(tri, BF16)
    row = lambda i: (i, 0)
    const = lambda i: (0, 0)
    in_specs = [pl.BlockSpec((tm, d), row), pl.BlockSpec((1, d), const),
                pl.BlockSpec((d, 3 * d), const), pl.BlockSpec((d, LANES), const),
                pl.BlockSpec((1, LANES), const), pl.BlockSpec((tm, tm), const)]
    args = [x2, g, wqkv, wf, bf, tri]
    out_shape = [jax.ShapeDtypeStruct((r, d), F32), jax.ShapeDtypeStruct((r, d), F32),
                 jax.ShapeDtypeStruct((r, LANES), F32)]
    out_specs = [pl.BlockSpec((tm, d), row), pl.BlockSpec((tm, d), row), pl.BlockSpec((tm, LANES), row)]
    if ext:
        selq, selk = _bias_selectors(n_heads, head_dim)
        in_specs += [pl.BlockSpec((LANES, n_heads * LANES), const)] * 2
        args += [selq, selk]
        out_shape += [jax.ShapeDtypeStruct((r, n_heads * LANES), BF16),
                      jax.ShapeDtypeStruct((r, n_heads * LANES), BF16),
                      jax.ShapeDtypeStruct((r, d), BF16)]
        out_specs += [pl.BlockSpec((tm, n_heads * LANES), row), pl.BlockSpec((tm, n_heads * LANES), row),
                      pl.BlockSpec((tm, d), row)]
    else:
        out_shape += [jax.ShapeDtypeStruct((r, LANES), F32), jax.ShapeDtypeStruct((r, d), F32)]
        out_specs += [pl.BlockSpec((tm, LANES), row), pl.BlockSpec((tm, d), row)]
    kern = functools.partial(_proj_kernel, tiles_per_seq=tiles_per_seq, n_heads=n_heads,
                             head_dim=head_dim, ext=ext)
    return pl.pallas_call(
        kern, grid=(r // tm,), in_specs=in_specs, out_specs=out_specs, out_shape=out_shape,
        scratch_shapes=[pltpu.VMEM((1, LANES), F32)],
        compiler_params=_params(1), name="proj_ext" if ext else "proj_dec")(*args)


def _attn_prompt_kernel(qe_ref, ke_ref, vb_ref, o_ref, m_ref, l_ref, acc_ref, *, head_dim):
    tq = qe_ref.shape[0]
    i = pl.program_id(2)
    row = lax.broadcasted_iota(jnp.int32, (tq, tq), 0)
    col = lax.broadcasted_iota(jnp.int32, (tq, tq), 1)
    halves = []
    for e in range(2):
        q = qe_ref[:, e * LANES:(e + 1) * LANES]
        m_ref[...] = jnp.full(m_ref.shape, -jnp.inf, F32)
        l_ref[...] = jnp.zeros_like(l_ref)
        acc_ref[...] = jnp.zeros_like(acc_ref)

        def step(j, masked, q=q, e=e):
            start = pl.multiple_of(j * tq, tq)
            k = ke_ref[pl.ds(start, tq), e * LANES:(e + 1) * LANES]
            v = vb_ref[pl.ds(start, tq), :]
            s = lax.dot_general(q, k, NT_DIMS, preferred_element_type=F32)
            if masked:
                s = jnp.where(col <= row, s, -jnp.inf)
            m_prev = m_ref[...]
            m_new = jnp.maximum(m_prev, jnp.max(s, axis=1, keepdims=True))
            p = jnp.exp(s - m_new)
            alpha = jnp.exp(m_prev - m_new)
            l_ref[...] = alpha * l_ref[...] + jnp.sum(p, axis=1, keepdims=True)
            acc_ref[...] = alpha * acc_ref[...] + jnp.dot(p.astype(BF16), v, preferred_element_type=F32)
            m_ref[...] = m_new

        def body(j, carry):
            step(j, False)
            return carry

        lax.fori_loop(0, i, body, 0)
        step(i, True)
        halves.append(acc_ref[...] / l_ref[...])
    lane = lax.broadcasted_iota(jnp.int32, halves[0].shape, 1)
    o_ref[...] = jnp.where(lane < head_dim, halves[0], halves[1]).astype(o_ref.dtype)


def _attn_prompt(qe, ke, vb, *, batch, seq_len, n_heads, head_dim, tq):
    r, d = vb.shape
    assert seq_len % tq == 0
    nq = seq_len // tq
    return pl.pallas_call(
        functools.partial(_attn_prompt_kernel, head_dim=head_dim),
        grid=(batch, n_heads // 2, nq),
        in_specs=[pl.BlockSpec((tq, 2 * LANES), lambda b, p, i: (b * nq + i, p)),
                  pl.BlockSpec((seq_len, 2 * LANES), lambda b, p, i: (b, p)),
                  pl.BlockSpec((seq_len, LANES), lambda b, p, i: (b, p))],
        out_specs=pl.BlockSpec((tq, LANES), lambda b, p, i: (b * nq + i, p)),
        out_shape=jax.ShapeDtypeStruct((r, d), BF16),
        scratch_shapes=[pltpu.VMEM((tq, 1), F32), pltpu.VMEM((tq, 1), F32), pltpu.VMEM((tq, LANES), F32)],
        compiler_params=_params(3), name="attn_prompt")(qe, ke, vb)


def _attn_paged_kernel(pt_ref, qs_ref, cnrow_ref, cnrep_ref, kn_ref, vn_ref, lt_ref, *rest,
                       pages, n_heads, head_dim):
    k_refs = rest[:pages]
    v_refs = rest[pages:2 * pages]
    lf_refs = rest[2 * pages:3 * pages]
    o_ref, qbd_ref, m_ref, l_ref, acc_ref, carry_ref = rest[3 * pages:]
    del pt_ref
    g = pl.program_id(1)
    t_new, d = kn_ref.shape[1], kn_ref.shape[2]
    rows = n_heads * t_new
    page = k_refs[0].shape[2]

    def online_update(s, v_bf16, v_transposed):
        m_prev = m_ref[...]
        m_new = jnp.maximum(m_prev, jnp.max(s, axis=1, keepdims=True))
        p = jnp.exp(s - m_new)
        alpha = jnp.exp(m_prev - m_new)
        l_ref[...] = alpha * l_ref[...] + jnp.sum(p, axis=1, keepdims=True)
        if v_transposed:
            pv = lax.dot_general(p.astype(BF16), v_bf16, NT_DIMS, preferred_element_type=F32)
        else:
            pv = jnp.dot(p.astype(BF16), v_bf16, preferred_element_type=F32)
        acc_ref[...] = alpha * acc_ref[...] + pv
        m_ref[...] = m_new

    @pl.when(g == 0)
    def _():
        qt = jnp.concatenate([qs_ref[0]] * n_heads, axis=0)
        r_i = lax.broadcasted_iota(jnp.int32, (rows, d), 0)
        l_i = lax.broadcasted_iota(jnp.int32, (rows, d), 1)
        qbd = jnp.where(r_i // t_new == l_i // head_dim, qt, 0.0).astype(BF16)
        qbd_ref[...] = qbd
        m_ref[...] = jnp.full(m_ref.shape, -jnp.inf, F32)
        l_ref[...] = jnp.zeros_like(l_ref)
        acc_ref[...] = jnp.zeros_like(acc_ref)
        carry_ref[...] = jnp.zeros_like(carry_ref)
        pad = jnp.zeros((LANES - t_new, d), F32)
        kn = jnp.concatenate([kn_ref[0], pad], axis=0).astype(BF16)
        vn = jnp.concatenate([vn_ref[0], pad], axis=0).astype(BF16)
        s = lax.dot_general(qbd, kn, NT_DIMS, preferred_element_type=F32)
        s = s + cnrow_ref[0] - cnrep_ref[0]
        r2 = lax.broadcasted_iota(jnp.int32, s.shape, 0)
        c2 = lax.broadcasted_iota(jnp.int32, s.shape, 1)
        s = jnp.where(c2 <= r2 % t_new, s, -jnp.inf)
        online_update(s, vn, False)

    qbd = qbd_ref[...]
    lt = lt_ref[...]
    for p in range(pages - 1, -1, -1):
        lft = lf_refs[p][0]
        suf = _dot3(lt, lft, sel_right=True) + carry_ref[...]
        carry_ref[...] = carry_ref[...] + jnp.sum(lft, axis=1, keepdims=True)
        bias = jnp.concatenate([jnp.broadcast_to(suf[h:h + 1, :], (t_new, page)) for h in range(n_heads)],
                               axis=0)
        s = jnp.dot(qbd, k_refs[p][0].astype(BF16), preferred_element_type=F32)
        s = s + cnrow_ref[0] + bias
        online_update(s, v_refs[p][0].astype(BF16), True)

    @pl.when(g == pl.num_programs(1) - 1)
    def _():
        full = acc_ref[...] / l_ref[...]
        out = jnp.zeros((t_new, d), F32)
        l_i = lax.broadcasted_iota(jnp.int32, (t_new, d), 1)
        for h in range(n_heads):
            out = jnp.where(l_i // head_dim == h, full[h * t_new:(h + 1) * t_new, :], out)
        o_ref[0] = out


def _attn_paged(page_table, qs, cn, k_new, v_new, cache_k, cache_v, cache_lf, *, n_heads, head_dim, pages):
    nseq, t_new, d = k_new.shape
    n_pool, page = cache_k.shape[0], cache_k.shape[1]
    n_pages = page_table.shape[1]
    rows = n_heads * t_new
    assert rows == LANES and page == LANES and n_pages % pages == 0
    steps = n_pages // pages
    cn_t = jnp.transpose(cn, (0, 2, 1))
    cnrow = cn_t.reshape(nseq, rows, 1)
    cnrep = jnp.repeat(cn_t, t_new, axis=1)
    cnrep = jnp.pad(cnrep, ((0, 0), (0, 0), (0, LANES - t_new)))
    lt = jnp.asarray(np.tril(np.ones((page, page), np.float32), -1), BF16)
    ck = jnp.transpose(cache_k, (0, 2, 3, 1)).reshape(n_pool, d, page)
    cv = jnp.transpose(cache_v, (0, 2, 3, 1)).reshape(n_pool, d, page)
    clf = jnp.transpose(cache_lf, (0, 2, 1))

    def seq_map(s, g, pt):
        return (s, 0, 0)

    def const_map(s, g, pt):
        return (0, 0)

    def page_map(i):
        def f(s, g, pt):
            return (pt[s, (steps - 1 - g) * pages + i], 0, 0)
        return f

    in_specs = [pl.BlockSpec((1, t_new, d), seq_map), pl.BlockSpec((1, rows, 1), seq_map),
                pl.BlockSpec((1, rows, LANES), seq_map), pl.BlockSpec((1, t_new, d), seq_map),
                pl.BlockSpec((1, t_new, d), seq_map), pl.BlockSpec((page, page), const_map)]
    in_specs += [pl.BlockSpec((1, d, page), page_map(i)) for i in range(pages)]
    in_specs += [pl.BlockSpec((1, d, page), page_map(i)) for i in range(pages)]
    in_specs += [pl.BlockSpec((1, n_heads, page), page_map(i)) for i in range(pages)]
    grid_spec = pltpu.PrefetchScalarGridSpec(
        num_scalar_prefetch=1, grid=(nseq, steps), in_specs=in_specs,
        out_specs=pl.BlockSpec((1, t_new, d), seq_map),
        scratch_shapes=[pltpu.VMEM((rows, d), BF16), pltpu.VMEM((rows, 1), F32), pltpu.VMEM((rows, 1), F32),
                        pltpu.VMEM((rows, d), F32), pltpu.VMEM((n_heads, 1), F32)])
    kern = functools.partial(_attn_paged_kernel, pages=pages, n_heads=n_heads, head_dim=head_dim)
    return pl.pallas_call(
        kern, grid_spec=grid_spec, out_shape=jax.ShapeDtypeStruct((nseq, t_new, d), F32),
        compiler_params=_params(2), name="attn_paged")(
            page_table, qs, cnrow, cnrep, k_new, v_new, lt,
            *([ck] * pages), *([cv] * pages), *([clf] * pages))


def _matmul_res_kernel(a_ref, w_ref, x_ref, o_ref):
    o_ref[...] = x_ref[...] + jnp.dot(a_ref[...], w_ref[...], preferred_element_type=F32)


def _matmul_res(a, w, x2, *, tm):
    r, d = x2.shape
    kdim = a.shape[1]
    assert r % tm == 0
    return pl.pallas_call(
        _matmul_res_kernel, grid=(r // tm,),
        in_specs=[pl.BlockSpec((tm, kdim), lambda i: (i, 0)), pl.BlockSpec((kdim, d), lambda i: (0, 0)),
                  pl.BlockSpec((tm, d), lambda i: (i, 0))],
        out_specs=pl.BlockSpec((tm, d), lambda i: (i, 0)),
        out_shape=jax.ShapeDtypeStruct((r, d), F32),
        compiler_params=_params(1), name="out_proj")(a, w, x2)


def _gated_kernel(*refs, kind, seq_len, final_norm, has_prev):
    it = iter(refs)
    x_ref, g_ref = next(it), next(it)
    up_refs = [next(it) for _ in range(2 if kind == "ffn" else 3)]
    dw_ref, wd_ref = next(it), next(it)
    gfin_ref = next(it) if final_norm else None
    p0_ref, p1_ref = (next(it), next(it)) if has_prev else (None, None)
    o_ref, st_ref, h_ref, acc_ref, ext_ref = next(it), next(it), next(it), next(it), next(it)
    tm = x_ref.shape[0]
    tf = up_refs[0].shape[1]
    r = pl.program_id(0)
    f = pl.program_id(1)
    pad = ext_ref.shape[1] - tm

    @pl.when(f == 0)
    def _():
        h_ref[...] = _rmsnorm(x_ref[...], g_ref[...]).astype(BF16)
        acc_ref[...] = jnp.zeros_like(acc_ref)

    h = h_ref[...]
    ups = [jnp.dot(h, w[...], preferred_element_type=F32) for w in up_refs]
    u = ups[0] if kind == "ffn" else ups[1] * ups[2]

    ext = ext_ref.at[f]
    tiles_per_seq = 1 if has_prev else seq_len // tm

    @pl.when(r % tiles_per_seq == 0)
    def _():
        ext[0:pad, :] = jnp.zeros((pad, tf), F32)

    ext[pad:pad + tm, :] = u
    u1 = ext[pad - 1:pad - 1 + tm, :]
    u2 = ext[pad - 2:pad - 2 + tm, :]
    ext[0:pad, :] = ext[tm:tm + pad, :]
    if has_prev:
        t = lax.broadcasted_iota(jnp.int32, (tm, tf), 0) % seq_len
        p0, p1 = p0_ref[...], p1_ref[...]
        u1 = jnp.where(t == 0, p1, u1)
        u2 = jnp.where(t == 0, p0, jnp.where(t == 1, p1, u2))
        st_ref[...] = u
    else:
        st_ref[0] = u[tm - (CONV_W - 1):, :]
    dw = dw_ref[...]
    conv = dw[0:1, :] * u2 + dw[1:2, :] * u1 + dw[2:3, :] * u
    if kind == "ffn":
        gated = conv * jax.nn.sigmoid(conv) * ups[1]
    else:
        gated = ups[0] * conv
    acc_ref[...] += jnp.dot(gated.astype(BF16), wd_ref[...], preferred_element_type=F32)

    @pl.when(f == pl.num_programs(1) - 1)
    def _():
        y = x_ref[...] + acc_ref[...]
        if final_norm:
            y = _rmsnorm(y, gfin_ref[...])
        o_ref[...] = y


def _gated(x2, g, ups, dw, wd, *, kind, seq_len, tm, tf, gfin=None, prev=None):
    r, d = x2.shape
    fdim = ups[0].shape[1]
    assert r % tm == 0 and fdim % tf == 0
    has_prev = prev is not None
    if has_prev:
        assert tm == r and tm % seq_len == 0 and seq_len >= CONV_W - 1
    else:
        assert seq_len % tm == 0
    nf = fdim // tf
    n_up = len(ups)
    pad = 8
    row = lambda i, j: (i, 0)
    const = lambda i, j: (0, 0)
    in_specs = [pl.BlockSpec((tm, d), row), pl.BlockSpec((1, d), const)]
    in_specs += [pl.BlockSpec((d, tf), lambda i, j: (0, j)) for _ in range(n_up)]
    in_specs += [pl.BlockSpec((CONV_W, tf), lambda i, j: (0, j)), pl.BlockSpec((tf, d), lambda i, j: (j, 0))]
    args = [x2, g, *ups, dw, wd]
    if gfin is not None:
        in_specs.append(pl.BlockSpec((1, d), const))
        args.append(gfin)
    if has_prev:
        in_specs += [pl.BlockSpec((tm, tf), lambda i, j: (i, j))] * 2
        args += list(prev)
        st_shape = jax.ShapeDtypeStruct((r, fdim), F32)
        st_spec = pl.BlockSpec((tm, tf), lambda i, j: (i, j))
    else:
        st_shape = jax.ShapeDtypeStruct((r // tm, CONV_W - 1, fdim), F32)
        st_spec = pl.BlockSpec((1, CONV_W - 1, tf), lambda i, j: (i, 0, j))
    kern = functools.partial(_gated_kernel, kind=kind, seq_len=seq_len, final_norm=gfin is not None,
                             has_prev=has_prev)
    return pl.pallas_call(
        kern, grid=(r // tm, nf), in_specs=in_specs,
        out_specs=[pl.BlockSpec((tm, d), row), st_spec],
        out_shape=[jax.ShapeDtypeStruct((r, d), F32), st_shape],
        scratch_shapes=[pltpu.VMEM((tm, d), BF16), pltpu.VMEM((tm, d), F32),
                        pltpu.VMEM((nf, tm + pad, tf), F32)],
        compiler_params=_params(2), name="gated_" + kind + ("_dec" if has_prev else ""))(*args)


def _tiles(seq_len, d_ff, d_model):
    def largest_divisor(n, cap, unit):
        best = unit
        for t in range(unit, min(n, cap) + 1, unit):
            if n % t == 0:
                best = t
        return best
    return dict(proj=largest_divisor(seq_len, 256, 8), attn=largest_divisor(seq_len, 512, 8),
                rows=largest_divisor(seq_len, 512, 8),
                ff=largest_divisor(d_ff, 1408, LANES), conv=largest_divisor(d_model, 512, LANES))


def _trunk(x3, *, paged, conv_prev, ffn_prev, w, cache, n_heads):
    b, t, d = x3.shape
    head_dim = d // n_heads
    r = b * t
    x2 = x3.reshape(r, d)
    d_ff = w["ffn_a"][0].shape[1]
    tl = _tiles(t, d_ff, d)
    if paged:
        tl = dict(tl, proj=r, rows=r)
    depth = len(w["ffn_a"])
    ks, vs, lfs, convs, ffns = [], [], [], [], []
    y = None
    for layer in range(depth):
        gm = w["norm_mix"][layer][None, :]
        if layer % 2 == 0:
            a = layer // 2
            outs = _proj(x2, gm, w["qkv"][a], w["wf"][a], w["bf"][a], seq_len=t, n_heads=n_heads,
                         head_dim=head_dim, tm=tl["proj"], ext=not paged)
            k, v, lf = outs[:3]
            if paged:
                cn, qs = outs[3], outs[4]
                o = _attn_paged(cache["page_table"], qs.reshape(b, t, d), cn[:, :n_heads].reshape(b, t, n_heads),
                                k.reshape(b, t, d), v.reshape(b, t, d), cache["k"][a], cache["v"][a],
                                cache["lf"][a], n_heads=n_heads, head_dim=head_dim, pages=cache["pages"])
                o = o.reshape(r, d).astype(BF16)
            else:
                qe, ke, vb = outs[3:]
                o = _attn_prompt(qe, ke, vb, batch=b, seq_len=t, n_heads=n_heads, head_dim=head_dim,
                                 tq=tl["attn"])
            x2 = _matmul_res(o, w["wo"][a], x2, tm=tl["rows"])
            ks.append(k.reshape(b, t, n_heads, head_dim))
            vs.append(v.reshape(b, t, n_heads, head_dim))
            lfs.append(lf[:, :n_heads].reshape(b, t, n_heads))
        else:
            c = layer // 2
            ups = [w["conv_in"][c][:, j * d:(j + 1) * d] for j in range(3)]
            prev = None
            if paged:
                prev = tuple(jnp.repeat(conv_prev[c][:, j], t, axis=0) for j in range(CONV_W - 1))
            x2, st = _gated(x2, gm, ups, w["conv_dw"][c], w["conv_out"][c], kind="conv", seq_len=t,
                            tm=tl["rows"], tf=tl["conv"], prev=prev)
            convs.append(_state_rows(st, b, t, paged, tl["rows"]))
        gf = w["norm_ffn"][layer][None, :]
        prev = None
        if paged:
            prev = tuple(jnp.repeat(ffn_prev[layer][:, j], t, axis=0) for j in range(CONV_W - 1))
        gfin = w["norm_final"][None, :] if layer == depth - 1 else None
        x2, st = _gated(x2, gf, [w["ffn_a"][layer], w["ffn_b"][layer]], w["ffn_dw"][layer],
                        w["ffn_down"][layer], kind="ffn", seq_len=t, tm=tl["rows"], tf=tl["ff"],
                        gfin=gfin, prev=prev)
        ffns.append(_state_rows(st, b, t, paged, tl["rows"]))
        y = x2
    return (y.reshape(b, t, d), jnp.stack(ks), jnp.stack(vs), jnp.stack(lfs), jnp.stack(convs),
            jnp.stack(ffns))


def _state_rows(st, b, t, paged, tm):
    if paged:
        return st.reshape(b, t, -1)[:, t - (CONV_W - 1):]
    tiles_per_seq = t // tm
    return st[tiles_per_seq - 1::tiles_per_seq]


def kernel(x_prompt, x_sample, cache_k, cache_v, cache_logf, state_conv, state_ffn, page_table, norm_mix, norm_ffn, norm_final, attn_w_qkv, attn_w_f, attn_b_f, attn_w_o, conv_w_in, conv_w_dw, conv_w_out, ffn_w_a, ffn_w_b, ffn_w_dw, ffn_w_down):
    n_attn, d, n_heads = attn_w_f.shape
    n_conv = conv_w_in.shape[0]
    depth = ffn_w_a.shape[0]
    wf = jnp.pad(attn_w_f, ((0, 0), (0, 0), (0, LANES - n_heads))).astype(BF16)
    bf = jnp.pad(attn_b_f, ((0, 0), (0, LANES - n_heads)))[:, None, :]
    w = dict(
        norm_mix=norm_mix, norm_ffn=norm_ffn, norm_final=norm_final,
        qkv=[attn_w_qkv[a].astype(BF16) for a in range(n_attn)],
        wf=[wf[a] for a in range(n_attn)], bf=[bf[a] for a in range(n_attn)],
        wo=[attn_w_o[a].astype(BF16) for a in range(n_attn)],
        conv_in=[conv_w_in[c].astype(BF16) for c in range(n_conv)],
        conv_dw=[conv_w_dw[c] for c in range(n_conv)],
        conv_out=[conv_w_out[c].astype(BF16) for c in range(n_conv)],
        ffn_a=[ffn_w_a[l].astype(BF16) for l in range(depth)],
        ffn_b=[ffn_w_b[l].astype(BF16) for l in range(depth)],
        ffn_dw=[ffn_w_dw[l] for l in range(depth)],
        ffn_down=[ffn_w_down[l].astype(BF16) for l in range(depth)])
    n_pages = page_table.shape[1]
    pages = 8 if n_pages % 8 == 0 else 1
    cache = dict(k=cache_k, v=cache_v, lf=cache_logf, page_table=page_table, pages=pages)
    prompt = _trunk(x_prompt, paged=False, conv_prev=None, ffn_prev=None, w=w, cache=None, n_heads=n_heads)
    sample = _trunk(x_sample, paged=True, conv_prev=state_conv, ffn_prev=state_ffn, w=w, cache=cache,
                    n_heads=n_heads)
    y_p, k_p, v_p, lf_p, conv_p, ffn_p = prompt
    y_s, k_s, v_s, lf_s, conv_s, ffn_s = sample
    return (y_p, y_s, k_p, v_p, lf_p, conv_p, ffn_p, k_s, v_s, lf_s, conv_s, ffn_s)
```

```python
import functools

import numpy as np
import jax
import jax.numpy as jnp
from jax import lax
from jax.experimental import pallas as pl
from jax.experimental.pallas import tpu as pltpu

F32 = jnp.float32
BF16 = jnp.bfloat16
NORM_EPS = 1e-6
CONV_W = 3
LANES = 128
V7X_VMEM_LIMIT = 56 * 1024 * 1024
NT_DIMS = (((1,), (1,)), ((), ()))
LOG2E = 1.4426950408889634
ONES_ROWS = 16
GATE_CHUNK = 256


def _params(n_axes):
    return pltpu.CompilerParams(dimension_semantics=("arbitrary",) * n_axes,
                                vmem_limit_bytes=V7X_VMEM_LIMIT)


def _rmsnorm(x, g):
    ms = jnp.mean(x * x, axis=-1, keepdims=True)
    return x * lax.rsqrt(ms + NORM_EPS) * g


def _log_sigmoid(z):
    return jnp.minimum(z, 0.0) - jnp.log1p(jnp.exp(-jnp.abs(z)))


def _split3(x):
    hi = x.astype(BF16)
    r1 = x - hi.astype(F32)
    mid = r1.astype(BF16)
    lo = (r1 - mid.astype(F32)).astype(BF16)
    return hi, mid, lo


def _dot3(sel, x, *, nt=False, sel_right=False):
    out = None
    for part in _split3(x):
        if sel_right:
            t = jnp.dot(part, sel, preferred_element_type=F32)
        elif nt:
            t = lax.dot_general(sel, part, NT_DIMS, preferred_element_type=F32)
        else:
            t = jnp.dot(sel, part, preferred_element_type=F32)
        out = t if out is None else out + t
    return out


def _proj_kernel(x_ref, g_ref, w_ref, wf_ref, bf_ref, tri_ref, *rest,
                 tiles_per_seq, n_heads, head_dim, ext):
    if ext:
        wvt_ref, selq_ref, selk_ref, kt_ref, vt_ref, lf_ref, qe_ref, ke_ref, vtb_ref, carry_ref = rest
    else:
        k_ref, v_ref, lf_ref, c_ref, qs_ref, carry_ref = rest
    tm, d = x_ref.shape
    i = pl.program_id(0)

    @pl.when(i % tiles_per_seq == 0)
    def _():
        carry_ref[...] = jnp.zeros_like(carry_ref)

    h = _rmsnorm(x_ref[...], g_ref[...]).astype(BF16)
    proj = jnp.dot(h, w_ref[...], preferred_element_type=F32)
    k = proj[:, d:2 * d]

    z = jnp.dot(h, wf_ref[...], preferred_element_type=F32) + bf_ref[...]
    lane = lax.broadcasted_iota(jnp.int32, z.shape, 1)
    lf = jnp.where(lane < n_heads, _log_sigmoid(z), 0.0)
    lf_ref[...] = lf

    c = _dot3(tri_ref[...], lf) + carry_ref[...]
    carry_ref[...] = c[tm - 1:tm, :]

    if not ext:
        k_ref[...] = k
        v_ref[...] = proj[:, 2 * d:]
        c_ref[...] = c
        qs_ref[...] = proj[:, :d] * (head_dim ** -0.5)
        return

    kt_ref[0] = k.T
    vt = lax.dot_general(wvt_ref[...], h, NT_DIMS, preferred_element_type=F32)
    vt_ref[0] = vt
    ones_rows = jnp.ones((ONES_ROWS, tm), F32)
    pieces = []
    for hd in range(n_heads):
        pieces += [vt[hd * head_dim:(hd + 1) * head_dim, :], ones_rows]
    vtb_ref[0, 0] = jnp.concatenate(pieces, axis=0).astype(BF16)

    q = proj[:, :d] * (head_dim ** -0.5 * LOG2E)
    c_hi, c_mid, c_lo = (p.astype(F32) for p in _split3(c * LOG2E))
    ones = (lane == 3 * n_heads).astype(F32)
    c3 = (c_hi + pltpu.roll(c_mid, n_heads, 1) + pltpu.roll(c_lo, 2 * n_heads, 1) + ones).astype(BF16)
    bq = jnp.dot(c3, selq_ref[...], preferred_element_type=F32)
    bk = jnp.dot(c3, selk_ref[...], preferred_element_type=F32)
    lane2 = lax.broadcasted_iota(jnp.int32, (1, bq.shape[1]), 1)
    own = (((lane2 // head_dim) + 1) & 2) == 0

    def dup(a):
        return jnp.concatenate([a[:, (j // 2) * LANES:(j // 2 + 1) * LANES] for j in range(n_heads)], axis=1)

    qe_ref[...] = jnp.where(own, dup(q), bq).astype(BF16)
    ke_ref[...] = jnp.where(own, dup(k), bk).astype(BF16)


def _bias_selectors(n_heads, head_dim):
    selq = np.zeros((LANES, n_heads * LANES), np.float32)
    selk = np.zeros((LANES, n_heads * LANES), np.float32)
    one = 3 * n_heads
    for h in range(n_heads):
        base = h * LANES + (head_dim if h % 2 == 0 else 0)
        for part in range(3):
            selq[part * n_heads + h, base + 2 * part] = 1.0
            selq[one, base + 2 * part + 1] = 1.0
            selk[one, base + 2 * part] = 1.0
            selk[part * n_heads + h, base + 2 * part + 1] = -1.0
    return jnp.asarray(selq, BF16), jnp.asarray(selk, BF16)


def _proj(x2, g, wqkv, wf, bf, *, seq_len, n_heads, head_dim, tm, tk=None):
    r, d = x2.shape
    ext = tk is not None
    assert 2 * head_dim == LANES and n_heads % 2 == 0 and 3 * n_heads < LANES
    assert r % tm == 0
    if seq_len >= tm:
        assert seq_len % tm == 0
        tiles_per_seq = seq_len // tm
        tri = np.tril(np.ones((tm, tm), np.float32))
    else:
        assert tm % seq_len == 0
        tiles_per_seq = 1
        idx = np.arange(tm)
        tri = ((idx[:, None] >= idx[None, :]) & (idx[:, None] // seq_len == idx[None, :] // seq_len))
        tri = tri.astype(np.float32)
    tri = jnp.asarray(tri, BF16)
    row = lambda i: (i, 0)
    const = lambda i: (0, 0)
    lf_shape, lf_spec = jax.ShapeDtypeStruct((r, LANES), F32), pl.BlockSpec((tm, LANES), row)
    if ext:
        assert seq_len % tk == 0 and tk % tm == 0
        batch, per_blk = r // seq_len, tk // tm
        selq, selk = _bias_selectors(n_heads, head_dim)
        w, wvt = wqkv[:, :2 * d], wqkv[:, 2 * d:].T
        in_specs = [pl.BlockSpec((tm, d), row), pl.BlockSpec((1, d), const),
                    pl.BlockSpec((d, 2 * d), const), pl.BlockSpec((d, LANES), const),
                    pl.BlockSpec((1, LANES), const), pl.BlockSpec((tm, tm), const),
                    pl.BlockSpec((d, d), const)] + [pl.BlockSpec((LANES, n_heads * LANES), const)] * 2
        args = [x2, g, w, wf, bf, tri, wvt, selq, selk]
        t_spec = pl.BlockSpec((1, d, tm), lambda i: (i // tiles_per_seq, 0, i % tiles_per_seq))
        out_shape = [jax.ShapeDtypeStruct((batch, d, seq_len), F32), jax.ShapeDtypeStruct((batch, d, seq_len), F32),
                     lf_shape,
                     jax.ShapeDtypeStruct((r, n_heads * LANES), BF16),
                     jax.ShapeDtypeStruct((r, n_heads * LANES), BF16),
                     jax.ShapeDtypeStruct((batch, seq_len // tk, n_heads * (head_dim + ONES_ROWS), tk), BF16)]
        out_specs = [t_spec, t_spec, lf_spec,
                     pl.BlockSpec((tm, n_heads * LANES), row), pl.BlockSpec((tm, n_heads * LANES), row),
                     pl.BlockSpec((1, 1, n_heads * (head_dim + ONES_ROWS), tm),
                                  lambda i: (i // tiles_per_seq, (i % tiles_per_seq) // per_blk, 0, i % per_blk))]
    else:
        in_specs = [pl.BlockSpec((tm, d), row), pl.BlockSpec((1, d), const),
                    pl.BlockSpec((d, 3 * d), const), pl.BlockSpec((d, LANES), const),
                    pl.BlockSpec((1, LANES), const), pl.BlockSpec((tm, tm), const)]
        args = [x2, g, wqkv, wf, bf, tri]
        out_shape = [jax.ShapeDtypeStruct((r, d), F32), jax.ShapeDtypeStruct((r, d), F32), lf_shape,
                     jax.ShapeDtypeStruct((r, LANES), F32), jax.ShapeDtypeStruct((r, d), F32)]
        out_specs = [pl.BlockSpec((tm, d), row), pl.BlockSpec((tm, d), row), lf_spec,
                     pl.BlockSpec((tm, LANES), row), pl.BlockSpec((tm, d), row)]
    kern = functools.partial(_proj_kernel, tiles_per_seq=tiles_per_seq, n_heads=n_heads,
                             head_dim=head_dim, ext=ext)
    return pl.pallas_call(
        kern, grid=(r // tm,), in_specs=in_specs, out_specs=out_specs, out_shape=out_shape,
        scratch_shapes=[pltpu.VMEM((1, LANES), F32)],
        compiler_params=_params(1), name="proj_ext" if ext else "proj_dec")(*args)


def _attn_prompt_kernel(qe_ref, ke_ref, vt_ref, o_ref, sa_ref, sb_ref, m_ref, acc_ref, *, head_dim):
    tq = qe_ref.shape[0]
    ts = sa_ref.shape[1]
    hr = acc_ref.shape[1]
    i = pl.program_id(2)
    kv_i = lax.broadcasted_iota(jnp.int32, (ts, tq), 0)
    q_i = lax.broadcasted_iota(jnp.int32, (ts, tq), 1)
    m_ref[...] = jnp.full(m_ref.shape, -jnp.inf, F32)
    acc_ref[...] = jnp.zeros_like(acc_ref)

    def scores(n, dst_ref):
        start = pl.multiple_of(n * ts, ts)
        for e in range(2):
            k = ke_ref[pl.ds(start, ts), e * LANES:(e + 1) * LANES]
            q = qe_ref[:, e * LANES:(e + 1) * LANES]
            dst_ref[e] = lax.dot_general(k, q, NT_DIMS, preferred_element_type=F32)

    def update(n, src_ref, diag):
        for e in range(2):
            st = src_ref[e]
            if diag is not None:
                st = jnp.where(kv_i + diag * ts <= q_i, st, -jnp.inf)
            m_prev = m_ref[e]
            m_new = jnp.maximum(m_prev, jnp.max(st, axis=0, keepdims=True))
            pt = jnp.exp2(st - m_new).astype(BF16)
            alpha = jnp.exp2(m_prev - m_new)
            vt = vt_ref[0, n, e * hr:(e + 1) * hr, :]
            acc_ref[e] = alpha * acc_ref[e] + jnp.dot(vt, pt, preferred_element_type=F32)
            m_ref[e] = m_new

    scores(0, sa_ref)

    def body(j, carry):
        scores(2 * j + 1, sb_ref)
        update(2 * j, sa_ref, None)
        scores(2 * j + 2, sa_ref)
        update(2 * j + 1, sb_ref, None)
        return carry

    lax.fori_loop(0, i, body, 0)
    scores(2 * i + 1, sb_ref)
    update(2 * i, sa_ref, 0)
    update(2 * i + 1, sb_ref, 1)
    outs = []
    for e in range(2):
        acc = acc_ref[e]
        outs.append(acc[:head_dim, :] * (1.0 / acc[head_dim:head_dim + 1, :]))
    o_ref[...] = jnp.concatenate(outs, axis=0).T.astype(o_ref.dtype)


def _attn_prompt(qe, ke, vtb, *, batch, seq_len, n_heads, head_dim, tq):
    nsl, vrows, ts = vtb.shape[1:]
    hr = vrows // n_heads
    assert seq_len % tq == 0 and tq == 2 * ts and nsl * ts == seq_len
    nq = seq_len // tq
    return pl.pallas_call(
        functools.partial(_attn_prompt_kernel, head_dim=head_dim),
        grid=(batch, n_heads // 2, nq),
        in_specs=[pl.BlockSpec((tq, 2 * LANES), lambda b, p, i: (b * nq + i, p)),
                  pl.BlockSpec((seq_len, 2 * LANES), lambda b, p, i: (b, p)),
                  pl.BlockSpec((1, nsl, 2 * hr, ts), lambda b, p, i: (b, 0, p, 0))],
        out_specs=pl.BlockSpec((tq, LANES), lambda b, p, i: (b * nq + i, p)),
        out_shape=jax.ShapeDtypeStruct((batch * seq_len, n_heads * head_dim), BF16),
        scratch_shapes=[pltpu.VMEM((2, ts, tq), F32), pltpu.VMEM((2, ts, tq), F32),
                        pltpu.VMEM((2, 1, tq), F32), pltpu.VMEM((2, hr, tq), F32)],
        compiler_params=_params(3), name="attn_prompt")(qe, ke, vtb)


def _attn_paged_kernel(pt_ref, qs_ref, cnrow_ref, cnrep_ref, kn_ref, vn_ref, lt_ref, *rest,
                       pages, n_heads, head_dim):
    k_refs = rest[:pages]
    v_refs = rest[pages:2 * pages]
    lf_refs = rest[2 * pages:3 * pages]
    o_ref, qbd_ref, m_ref, l_ref, acc_ref, carry_ref = rest[3 * pages:]
    del pt_ref
    g = pl.program_id(1)
    t_new, d = kn_ref.shape[1], kn_ref.shape[2]
    rows = n_heads * t_new
    page = k_refs[0].shape[2]

    def online_update(s, v_bf16, v_transposed):
        m_prev = m_ref[...]
        m_new = jnp.maximum(m_prev, jnp.max(s, axis=1, keepdims=True))
        p = jnp.exp(s - m_new)
        alpha = jnp.exp(m_prev - m_new)
        l_ref[...] = alpha * l_ref[...] + jnp.sum(p, axis=1, keepdims=True)
        if v_transposed:
            pv = lax.dot_general(p.astype(BF16), v_bf16, NT_DIMS, preferred_element_type=F32)
        else:
            pv = jnp.dot(p.astype(BF16), v_bf16, preferred_element_type=F32)
        acc_ref[...] = alpha * acc_ref[...] + pv
        m_ref[...] = m_new

    @pl.when(g == 0)
    def _():
        qt = jnp.concatenate([qs_ref[0]] * n_heads, axis=0)
        r_i = lax.broadcasted_iota(jnp.int32, (rows, d), 0)
        l_i = lax.broadcasted_iota(jnp.int32, (rows, d), 1)
        qbd = jnp.where(r_i // t_new == l_i // head_dim, qt, 0.0).astype(BF16)
        qbd_ref[...] = qbd
        m_ref[...] = jnp.full(m_ref.shape, -jnp.inf, F32)
        l_ref[...] = jnp.zeros_like(l_ref)
        acc_ref[...] = jnp.zeros_like(acc_ref)
        carry_ref[...] = jnp.zeros_like(carry_ref)
        pad = jnp.zeros((LANES - t_new, d), F32)
        kn = jnp.concatenate([kn_ref[0], pad], axis=0).astype(BF16)
        vn = jnp.concatenate([vn_ref[0], pad], axis=0).astype(BF16)
        s = lax.dot_general(qbd, kn, NT_DIMS, preferred_element_type=F32)
        s = s + cnrow_ref[0] - cnrep_ref[0]
        r2 = lax.broadcasted_iota(jnp.int32, s.shape, 0)
        c2 = lax.broadcasted_iota(jnp.int32, s.shape, 1)
        s = jnp.where(c2 <= r2 % t_new, s, -jnp.inf)
        online_update(s, vn, False)

    lt = lt_ref[...]
    run = carry_ref[...]
    sufs = [None] * pages
    for p in range(pages - 1, -1, -1):
        lft = lf_refs[p][0]
        sufs[p] = _dot3(lt, lft, sel_right=True) + run
        run = run + jnp.sum(lft, axis=1, keepdims=True)
    carry_ref[...] = run
    suf = jnp.concatenate(sufs, axis=1)
    bias = jnp.concatenate([jnp.broadcast_to(suf[h:h + 1, :], (t_new, pages * page)) for h in range(n_heads)],
                           axis=0)
    kt = jnp.concatenate([k_refs[p][0].astype(BF16) for p in range(pages)], axis=1)
    vt = jnp.concatenate([v_refs[p][0].astype(BF16) for p in range(pages)], axis=1)
    s = jnp.dot(qbd_ref[...], kt, preferred_element_type=F32) + cnrow_ref[0] + bias
    online_update(s, vt, True)

    @pl.when(g == pl.num_programs(1) - 1)
    def _():
        full = acc_ref[...] / l_ref[...]
        out = jnp.zeros((t_new, d), F32)
        l_i = lax.broadcasted_iota(jnp.int32, (t_new, d), 1)
        for h in range(n_heads):
            out = jnp.where(l_i // head_dim == h, full[h * t_new:(h + 1) * t_new, :], out)
        o_ref[0] = out


def _attn_paged(page_table, qs, cn, k_new, v_new, cache_k, cache_v, cache_lf, *, n_heads, head_dim, pages):
    nseq, t_new, d = k_new.shape
    n_pool, page = cache_k.shape[0], cache_k.shape[1]
    n_pages = page_table.shape[1]
    rows = n_heads * t_new
    assert rows == LANES and page == LANES and n_pages % pages == 0
    steps = n_pages // pages
    cn_t = jnp.transpose(cn, (0, 2, 1))
    cnrow = cn_t.reshape(nseq, rows, 1)
    cnrep = jnp.repeat(cn_t, t_new, axis=1)
    cnrep = jnp.pad(cnrep, ((0, 0), (0, 0), (0, LANES - t_new)))
    lt = jnp.asarray(np.tril(np.ones((page, page), np.float32), -1), BF16)
    ck = jnp.transpose(cache_k, (0, 2, 3, 1)).reshape(n_pool, d, page)
    cv = jnp.transpose(cache_v, (0, 2, 3, 1)).reshape(n_pool, d, page)
    clf = jnp.transpose(cache_lf, (0, 2, 1))

    def seq_map(s, g, pt):
        return (s, 0, 0)

    def const_map(s, g, pt):
        return (0, 0)

    def page_map(i):
        def f(s, g, pt):
            return (pt[s, (steps - 1 - g) * pages + i], 0, 0)
        return f

    in_specs = [pl.BlockSpec((1, t_new, d), seq_map), pl.BlockSpec((1, rows, 1), seq_map),
                pl.BlockSpec((1, rows, LANES), seq_map), pl.BlockSpec((1, t_new, d), seq_map),
                pl.BlockSpec((1, t_new, d), seq_map), pl.BlockSpec((page, page), const_map)]
    in_specs += [pl.BlockSpec((1, d, page), page_map(i)) for i in range(pages)]
    in_specs += [pl.BlockSpec((1, d, page), page_map(i)) for i in range(pages)]
    in_specs += [pl.BlockSpec((1, n_heads, page), page_map(i)) for i in range(pages)]
    grid_spec = pltpu.PrefetchScalarGridSpec(
        num_scalar_prefetch=1, grid=(nseq, steps), in_specs=in_specs,
        out_specs=pl.BlockSpec((1, t_new, d), seq_map),
        scratch_shapes=[pltpu.VMEM((rows, d), BF16), pltpu.VMEM((rows, 1), F32), pltpu.VMEM((rows, 1), F32),
                        pltpu.VMEM((rows, d), F32), pltpu.VMEM((n_heads, 1), F32)])
    kern = functools.partial(_attn_paged_kernel, pages=pages, n_heads=n_heads, head_dim=head_dim)
    return pl.pallas_call(
        kern, grid_spec=grid_spec, out_shape=jax.ShapeDtypeStruct((nseq, t_new, d), F32),
        compiler_params=_params(2), name="attn_paged")(
            page_table, qs, cnrow, cnrep, k_new, v_new, lt,
            *([ck] * pages), *([cv] * pages), *([clf] * pages))


def _matmul_res_kernel(a_ref, w_ref, x_ref, o_ref):
    o_ref[...] = x_ref[...] + jnp.dot(a_ref[...], w_ref[...], preferred_element_type=F32)


def _matmul_res(a, w, x2, *, tm):
    r, d = x2.shape
    kdim = a.shape[1]
    assert r % tm == 0
    return pl.pallas_call(
        _matmul_res_kernel, grid=(r // tm,),
        in_specs=[pl.BlockSpec((tm, kdim), lambda i: (i, 0)), pl.BlockSpec((kdim, d), lambda i: (0, 0)),
                  pl.BlockSpec((tm, d), lambda i: (i, 0))],
        out_specs=pl.BlockSpec((tm, d), lambda i: (i, 0)),
        out_shape=jax.ShapeDtypeStruct((r, d), F32),
        compiler_params=_params(1), name="out_proj")(a, w, x2)


def _gated_kernel(*refs, kind, seq_len, final_norm, has_prev):
    it = iter(refs)
    x_ref, g_ref = next(it), next(it)
    up_refs = [next(it) for _ in range(2 if kind == "ffn" else 3)]
    dw_ref, wd_ref = next(it), next(it)
    gfin_ref = next(it) if final_norm else None
    p0_ref, p1_ref = (next(it), next(it)) if has_prev else (None, None)
    o_ref, st_ref, h_ref, gate_ref, ext_ref = next(it), next(it), next(it), next(it), next(it)
    acc_ref = next(it, None)
    tm = x_ref.shape[0]
    tf = up_refs[0].shape[1]
    r = pl.program_id(0)
    f = pl.program_id(1)
    pad = ext_ref.shape[1] - tm

    @pl.when(f == 0)
    def _():
        h_ref[...] = _rmsnorm(x_ref[...], g_ref[...]).astype(BF16)

    h = h_ref[...]
    ext = ext_ref.at[f]
    tiles_per_seq = 1 if has_prev else seq_len // tm

    @pl.when(r % tiles_per_seq == 0)
    def _():
        ext[0:pad, :] = jnp.zeros((pad, tf), F32)

    for c0 in range(0, tf, GATE_CHUNK):
        cols = slice(c0, min(c0 + GATE_CHUNK, tf))
        ups = [jnp.dot(h, w[:, cols], preferred_element_type=F32) for w in up_refs]
        u = ups[0] if kind == "ffn" else ups[1] * ups[2]
        ext[pad:pad + tm, cols] = u
        u1 = ext[pad - 1:pad - 1 + tm, cols]
        u2 = ext[pad - 2:pad - 2 + tm, cols]
        ext[0:pad, cols] = ext[tm:tm + pad, cols]
        if has_prev:
            t = lax.broadcasted_iota(jnp.int32, u.shape, 0) % seq_len
            p0, p1 = p0_ref[:, cols], p1_ref[:, cols]
            u1 = jnp.where(t == 0, p1, u1)
            u2 = jnp.where(t == 0, p0, jnp.where(t == 1, p1, u2))
            st_ref[:, cols] = u
        else:
            st_ref[0, :, cols] = u[tm - (CONV_W - 1):, :]
        conv = dw_ref[0:1, cols] * u2 + dw_ref[1:2, cols] * u1 + dw_ref[2:3, cols] * u
        if kind == "ffn":
            gated = conv * jax.nn.sigmoid(conv) * ups[1]
        else:
            gated = ups[0] * conv
        gate_ref[:, cols] = gated.astype(BF16)

    down = jnp.dot(gate_ref[...], wd_ref[...], preferred_element_type=F32)
    last = pl.num_programs(1) - 1
    if acc_ref is not None:
        @pl.when(f == 0)
        def _():
            acc_ref[...] = down

        @pl.when(jnp.logical_and(f > 0, f < last))
        def _():
            acc_ref[...] += down

    @pl.when(f == last)
    def _():
        y = x_ref[...] + down
        if acc_ref is not None:
            y = y + acc_ref[...]
        if final_norm:
            y = _rmsnorm(y, gfin_ref[...])
        o_ref[...] = y


def _gated(x2, g, ups, dw, wd, *, kind, seq_len, tm, tf, gfin=None, prev=None):
    r, d = x2.shape
    fdim = ups[0].shape[1]
    assert r % tm == 0 and fdim % tf == 0
    has_prev = prev is not None
    if has_prev:
        assert tm == r and tm % seq_len == 0 and seq_len >= CONV_W - 1
    else:
        assert seq_len % tm == 0
    nf = fdim // tf
    n_up = len(ups)
    pad = 8
    row = lambda i, j: (i, 0)
    const = lambda i, j: (0, 0)
    once = dict(pipeline_mode=pl.Buffered(1)) if nf == 1 else {}
    in_specs = [pl.BlockSpec((tm, d), row), pl.BlockSpec((1, d), const)]
    in_specs += [pl.BlockSpec((d, tf), lambda i, j: (0, j), **once) for _ in range(n_up)]
    in_specs += [pl.BlockSpec((CONV_W, tf), lambda i, j: (0, j)),
                 pl.BlockSpec((tf, d), lambda i, j: (j, 0), **once)]
    args = [x2, g, *ups, dw, wd]
    if gfin is not None:
        in_specs.append(pl.BlockSpec((1, d), const))
        args.append(gfin)
    if has_prev:
        in_specs += [pl.BlockSpec((tm, tf), lambda i, j: (i, j))] * 2
        args += list(prev)
        st_shape = jax.ShapeDtypeStruct((r, fdim), F32)
        st_spec = pl.BlockSpec((tm, tf), lambda i, j: (i, j))
    else:
        st_shape = jax.ShapeDtypeStruct((r // tm, CONV_W - 1, fdim), F32)
        st_spec = pl.BlockSpec((1, CONV_W - 1, tf), lambda i, j: (i, 0, j))
    kern = functools.partial(_gated_kernel, kind=kind, seq_len=seq_len, final_norm=gfin is not None,
                             has_prev=has_prev)
    return pl.pallas_call(
        kern, grid=(r // tm, nf), in_specs=in_specs,
        out_specs=[pl.BlockSpec((tm, d), row), st_spec],
        out_shape=[jax.ShapeDtypeStruct((r, d), F32), st_shape],
        scratch_shapes=[pltpu.VMEM((tm, d), BF16), pltpu.VMEM((tm, tf), BF16),
                        pltpu.VMEM((nf, tm + pad, tf), F32)] + ([pltpu.VMEM((tm, d), F32)] if nf > 1 else []),
        compiler_params=_params(2), name="gated_" + kind + ("_dec" if has_prev else ""))(*args)


def _tiles(seq_len, d_ff, d_model):
    def largest_divisor(n, cap, unit):
        best = unit
        for t in range(unit, min(n, cap) + 1, unit):
            if n % t == 0:
                best = t
        return best
    return dict(proj=largest_divisor(seq_len, 256, 8), attn=largest_divisor(seq_len, 512, 2 * LANES),
                rows=largest_divisor(seq_len, 512, 8),
                ff=d_ff, conv=d_model)


def _trunk(x3, *, paged, conv_prev, ffn_prev, w, cache, n_heads):
    b, t, d = x3.shape
    head_dim = d // n_heads
    r = b * t
    x2 = x3.reshape(r, d)
    d_ff = w["ffn_a"][0].shape[1]
    tl = _tiles(t, d_ff, d)
    if paged:
        tl = dict(tl, proj=r, rows=r)
    depth = len(w["ffn_a"])
    ks, vs, lfs, convs, ffns = [], [], [], [], []
    y = None
    for layer in range(depth):
        gm = w["norm_mix"][layer][None, :]
        if layer % 2 == 0:
            a = layer // 2
            if paged:
                k, v, lf, cn, qs = _proj(x2, gm, w["qkv"][a], w["wf"][a], w["bf"][a], seq_len=t,
                                         n_heads=n_heads, head_dim=head_dim, tm=tl["proj"])
                o = _attn_paged(cache["page_table"], qs.reshape(b, t, d), cn[:, :n_heads].reshape(b, t, n_heads),
                                k.reshape(b, t, d), v.reshape(b, t, d), cache["k"][a], cache["v"][a],
                                cache["lf"][a], n_heads=n_heads, head_dim=head_dim, pages=cache["pages"])
                o = o.reshape(r, d).astype(BF16)
                ks.append(k.reshape(b, t, n_heads, head_dim))
                vs.append(v.reshape(b, t, n_heads, head_dim))
            else:
                kt, vt, lf, qe, ke, vtb = _proj(x2, gm, w["qkv"][a], w["wf"][a], w["bf"][a], seq_len=t,
                                                n_heads=n_heads, head_dim=head_dim, tm=tl["proj"],
                                                tk=tl["attn"] // 2)
                o = _attn_prompt(qe, ke, vtb, batch=b, seq_len=t, n_heads=n_heads, head_dim=head_dim,
                                 tq=tl["attn"])
                ks.append(jnp.transpose(kt.reshape(b, n_heads, head_dim, t), (0, 3, 1, 2)))
                vs.append(jnp.transpose(vt.reshape(b, n_heads, head_dim, t), (0, 3, 1, 2)))
            x2 = _matmul_res(o, w["wo"][a], x2, tm=tl["rows"])
            lfs.append(lf[:, :n_heads].reshape(b, t, n_heads))
        else:
            c = layer // 2
            ups = [w["conv_in"][c][:, j * d:(j + 1) * d] for j in range(3)]
            prev = None
            if paged:
                prev = tuple(jnp.repeat(conv_prev[c][:, j], t, axis=0) for j in range(CONV_W - 1))
            x2, st = _gated(x2, gm, ups, w["conv_dw"][c], w["conv_out"][c], kind="conv", seq_len=t,
                            tm=tl["rows"], tf=tl["conv"], prev=prev)
            convs.append(_state_rows(st, b, t, paged, tl["rows"]))
        gf = w["norm_ffn"][layer][None, :]
        prev = None
        if paged:
            prev = tuple(jnp.repeat(ffn_prev[layer][:, j], t, axis=0) for j in range(CONV_W - 1))
        gfin = w["norm_final"][None, :] if layer == depth - 1 else None
        x2, st = _gated(x2, gf, [w["ffn_a"][layer], w["ffn_b"][layer]], w["ffn_dw"][layer],
                        w["ffn_down"][layer], kind="ffn", seq_len=t, tm=tl["rows"], tf=tl["ff"],
                        gfin=gfin, prev=prev)
        ffns.append(_state_rows(st, b, t, paged, tl["rows"]))
        y = x2
    return (y.reshape(b, t, d), jnp.stack(ks), jnp.stack(vs), jnp.stack(lfs), jnp.stack(convs),
            jnp.stack(ffns))


def _state_rows(st, b, t, paged, tm):
    if paged:
        return st.reshape(b, t, -1)[:, t - (CONV_W - 1):]
    tiles_per_seq = t // tm
    return st[tiles_per_seq - 1::tiles_per_seq]


def kernel(x_prompt, x_sample, cache_k, cache_v, cache_logf, state_conv, state_ffn, page_table, norm_mix, norm_ffn, norm_final, attn_w_qkv, attn_w_f, attn_b_f, attn_w_o, conv_w_in, conv_w_dw, conv_w_out, ffn_w_a, ffn_w_b, ffn_w_dw, ffn_w_down):
    n_attn, d, n_heads = attn_w_f.shape
    n_conv = conv_w_in.shape[0]
    depth = ffn_w_a.shape[0]
    wf = jnp.pad(attn_w_f, ((0, 0), (0, 0), (0, LANES - n_heads))).astype(BF16)
    bf = jnp.pad(attn_b_f, ((0, 0), (0, LANES - n_heads)))[:, None, :]
    w = dict(
        norm_mix=norm_mix, norm_ffn=norm_ffn, norm_final=norm_final,
        qkv=[attn_w_qkv[a].astype(BF16) for a in range(n_attn)],
        wf=[wf[a] for a in range(n_attn)], bf=[bf[a] for a in range(n_attn)],
        wo=[attn_w_o[a].astype(BF16) for a in range(n_attn)],
        conv_in=[conv_w_in[c].astype(BF16) for c in range(n_conv)],
        conv_dw=[conv_w_dw[c] for c in range(n_conv)],
        conv_out=[conv_w_out[c].astype(BF16) for c in range(n_conv)],
        ffn_a=[ffn_w_a[l].astype(BF16) for l in range(depth)],
        ffn_b=[ffn_w_b[l].astype(BF16) for l in range(depth)],
        ffn_dw=[ffn_w_dw[l] for l in range(depth)],
        ffn_down=[ffn_w_down[l].astype(BF16) for l in range(depth)])
    n_pages = page_table.shape[1]
    pages = 8 if n_pages % 8 == 0 else 1
    cache = dict(k=cache_k, v=cache_v, lf=cache_logf, page_table=page_table, pages=pages)
    prompt = _trunk(x_prompt, paged=False, conv_prev=None, ffn_prev=None, w=w, cache=None, n_heads=n_heads)
    sample = _trunk(x_sample, paged=True, conv_prev=state_conv, ffn_prev=state_ffn, w=w, cache=cache,
                    n_heads=n_heads)
    y_p, k_p, v_p, lf_p, conv_p, ffn_p = prompt
    y_s, k_s, v_s, lf_s, conv_s, ffn_s = sample
    return (y_p, y_s, k_p, v_p, lf_p, conv_p, ffn_p, k_s, v_s, lf_s, conv_s, ffn_s)
```

```python
import functools

import numpy as np
import jax
import jax.numpy as jnp
from jax import lax
from jax.experimental import pallas as pl
from jax.experimental.pallas import tpu as pltpu

F32 = jnp.float32
BF16 = jnp.bfloat16
NORM_EPS = 1e-6
CONV_W = 3
LANES = 128
V7X_VMEM_LIMIT = 56 * 1024 * 1024
NT_DIMS = (((1,), (1,)), ((), ()))
LOG2E = 1.4426950408889634
ONES_ROWS = 16
GATE_CHUNK = 256


def _params(n_axes):
    return pltpu.CompilerParams(dimension_semantics=("arbitrary",) * n_axes,
                                vmem_limit_bytes=V7X_VMEM_LIMIT)


def _rmsnorm(x, g):
    ms = jnp.mean(x * x, axis=-1, keepdims=True)
    return x * lax.rsqrt(ms + NORM_EPS) * g


def _log_sigmoid(z):
    return jnp.minimum(z, 0.0) - jnp.log1p(jnp.exp(-jnp.abs(z)))


def _split3(x):
    hi = x.astype(BF16)
    r1 = x - hi.astype(F32)
    mid = r1.astype(BF16)
    lo = (r1 - mid.astype(F32)).astype(BF16)
    return hi, mid, lo


def _dot3(sel, x, *, nt=False, sel_right=False):
    out = None
    for part in _split3(x):
        if sel_right:
            t = jnp.dot(part, sel, preferred_element_type=F32)
        elif nt:
            t = lax.dot_general(sel, part, NT_DIMS, preferred_element_type=F32)
        else:
            t = jnp.dot(sel, part, preferred_element_type=F32)
        out = t if out is None else out + t
    return out


def _proj_kernel(x_ref, g_ref, w_ref, wf_ref, bf_ref, tri_ref, *rest,
                 tiles_per_seq, n_heads, head_dim, ext):
    if ext:
        wvt_ref, selq_ref, selk_ref, kt_ref, vt_ref, lf_ref, qe_ref, ke_ref, vtb_ref, carry_ref = rest
    else:
        k_ref, v_ref, lf_ref, c_ref, qs_ref, carry_ref = rest
    tm, d = x_ref.shape
    i = pl.program_id(0)

    @pl.when(i % tiles_per_seq == 0)
    def _():
        carry_ref[...] = jnp.zeros_like(carry_ref)

    h = _rmsnorm(x_ref[...], g_ref[...]).astype(BF16)
    proj = jnp.dot(h, w_ref[...], preferred_element_type=F32)
    k = proj[:, d:2 * d]

    z = jnp.dot(h, wf_ref[...], preferred_element_type=F32) + bf_ref[...]
    lane = lax.broadcasted_iota(jnp.int32, z.shape, 1)
    lf = jnp.where(lane < n_heads, _log_sigmoid(z), 0.0)
    lf_ref[...] = lf

    c = _dot3(tri_ref[...], lf) + carry_ref[...]
    carry_ref[...] = c[tm - 1:tm, :]

    if not ext:
        k_ref[...] = k
        v_ref[...] = proj[:, 2 * d:]
        c_ref[...] = c
        qs_ref[...] = proj[:, :d] * (head_dim ** -0.5)
        return

    kt_ref[0] = k.T
    vt = lax.dot_general(wvt_ref[...], h, NT_DIMS, preferred_element_type=F32)
    vt_ref[0] = vt
    ones_rows = jnp.ones((ONES_ROWS, tm), F32)
    pieces = []
    for hd in range(n_heads):
        pieces += [vt[hd * head_dim:(hd + 1) * head_dim, :], ones_rows]
    vtb_ref[0, 0] = jnp.concatenate(pieces, axis=0).astype(BF16)

    q = proj[:, :d] * (head_dim ** -0.5 * LOG2E)
    c_hi, c_mid, c_lo = (p.astype(F32) for p in _split3(c * LOG2E))
    ones = (lane == 3 * n_heads).astype(F32)
    c3 = (c_hi + pltpu.roll(c_mid, n_heads, 1) + pltpu.roll(c_lo, 2 * n_heads, 1) + ones).astype(BF16)
    bq = jnp.dot(c3, selq_ref[...], preferred_element_type=F32)
    bk = jnp.dot(c3, selk_ref[...], preferred_element_type=F32)
    lane2 = lax.broadcasted_iota(jnp.int32, (1, bq.shape[1]), 1)
    own = (((lane2 // head_dim) + 1) & 2) == 0

    def dup(a):
        return jnp.concatenate([a[:, (j // 2) * LANES:(j // 2 + 1) * LANES] for j in range(n_heads)], axis=1)

    qe_ref[...] = jnp.where(own, dup(q), bq).astype(BF16)
    ke_ref[...] = jnp.where(own, dup(k), bk).astype(BF16)


def _bias_selectors(n_heads, head_dim):
    selq = np.zeros((LANES, n_heads * LANES), np.float32)
    selk = np.zeros((LANES, n_heads * LANES), np.float32)
    one = 3 * n_heads
    for h in range(n_heads):
        base = h * LANES + (head_dim if h % 2 == 0 else 0)
        for part in range(3):
            selq[part * n_heads + h, base + 2 * part] = 1.0
            selq[one, base + 2 * part + 1] = 1.0
            selk[one, base + 2 * part] = 1.0
            selk[part * n_heads + h, base + 2 * part + 1] = -1.0
    return jnp.asarray(selq, BF16), jnp.asarray(selk, BF16)


def _proj(x2, g, wqkv, wf, bf, *, seq_len, n_heads, head_dim, tm, tk=None):
    r, d = x2.shape
    ext = tk is not None
    assert 2 * head_dim == LANES and n_heads % 2 == 0 and 3 * n_heads < LANES
    assert r % tm == 0
    if seq_len >= tm:
        assert seq_len % tm == 0
        tiles_per_seq = seq_len // tm
        tri = np.tril(np.ones((tm, tm), np.float32))
    else:
        assert tm % seq_len == 0
        tiles_per_seq = 1
        idx = np.arange(tm)
        tri = ((idx[:, None] >= idx[None, :]) & (idx[:, None] // seq_len == idx[None, :] // seq_len))
        tri = tri.astype(np.float32)
    tri = jnp.asarray(tri, BF16)
    row = lambda i: (i, 0)
    const = lambda i: (0, 0)
    lf_shape, lf_spec = jax.ShapeDtypeStruct((r, LANES), F32), pl.BlockSpec((tm, LANES), row)
    if ext:
        assert seq_len % tk == 0 and tk % tm == 0
        batch, per_blk = r // seq_len, tk // tm
        selq, selk = _bias_selectors(n_heads, head_dim)
        w, wvt = wqkv[:, :2 * d], wqkv[:, 2 * d:].T
        in_specs = [pl.BlockSpec((tm, d), row), pl.BlockSpec((1, d), const),
                    pl.BlockSpec((d, 2 * d), const), pl.BlockSpec((d, LANES), const),
                    pl.BlockSpec((1, LANES), const), pl.BlockSpec((tm, tm), const),
                    pl.BlockSpec((d, d), const)] + [pl.BlockSpec((LANES, n_heads * LANES), const)] * 2
        args = [x2, g, w, wf, bf, tri, wvt, selq, selk]
        t_spec = pl.BlockSpec((1, d, tm), lambda i: (i // tiles_per_seq, 0, i % tiles_per_seq))
        out_shape = [jax.ShapeDtypeStruct((batch, d, seq_len), F32), jax.ShapeDtypeStruct((batch, d, seq_len), F32),
                     lf_shape,
                     jax.ShapeDtypeStruct((r, n_heads * LANES), BF16),
                     jax.ShapeDtypeStruct((r, n_heads * LANES), BF16),
                     jax.ShapeDtypeStruct((batch, seq_len // tk, n_heads * (head_dim + ONES_ROWS), tk), BF16)]
        out_specs = [t_spec, t_spec, lf_spec,
                     pl.BlockSpec((tm, n_heads * LANES), row), pl.BlockSpec((tm, n_heads * LANES), row),
                     pl.BlockSpec((1, 1, n_heads * (head_dim + ONES_ROWS), tm),
                                  lambda i: (i // tiles_per_seq, (i % tiles_per_seq) // per_blk, 0, i % per_blk))]
    else:
        in_specs = [pl.BlockSpec((tm, d), row), pl.BlockSpec((1, d), const),
                    pl.BlockSpec((d, 3 * d), const), pl.BlockSpec((d, LANES), const),
                    pl.BlockSpec((1, LANES), const), pl.BlockSpec((tm, tm), const)]
        args = [x2, g, wqkv, wf, bf, tri]
        out_shape = [jax.ShapeDtypeStruct((r, d), F32), jax.ShapeDtypeStruct((r, d), F32), lf_shape,
                     jax.ShapeDtypeStruct((r, LANES), F32), jax.ShapeDtypeStruct((r, d), F32)]
        out_specs = [pl.BlockSpec((tm, d), row), pl.BlockSpec((tm, d), row), lf_spec,
                     pl.BlockSpec((tm, LANES), row), pl.BlockSpec((tm, d), row)]
    kern = functools.partial(_proj_kernel, tiles_per_seq=tiles_per_seq, n_heads=n_heads,
                             head_dim=head_dim, ext=ext)
    return pl.pallas_call(
        kern, grid=(r // tm,), in_specs=in_specs, out_specs=out_specs, out_shape=out_shape,
        scratch_shapes=[pltpu.VMEM((1, LANES), F32)],
        compiler_params=_params(1), name="proj_ext" if ext else "proj_dec")(*args)


def _attn_prompt_kernel(qe_ref, ke_ref, vt_ref, o_ref, sa_ref, sb_ref, m_ref, acc_ref, *, head_dim):
    tq = o_ref.shape[0]
    ts = sa_ref.shape[1]
    hr = acc_ref.shape[1]
    i = pl.program_id(2)
    kv_i = lax.broadcasted_iota(jnp.int32, (ts, tq), 0)
    q_i = lax.broadcasted_iota(jnp.int32, (ts, tq), 1)
    m_ref[...] = jnp.full(m_ref.shape, -jnp.inf, F32)
    acc_ref[...] = jnp.zeros_like(acc_ref)

    def scores(qi, n, dst_ref):
        k0 = pl.multiple_of(n * ts, ts)
        q0 = pl.multiple_of(qi * tq, tq)
        for e in range(2):
            k = ke_ref[pl.ds(k0, ts), e * LANES:(e + 1) * LANES]
            q = qe_ref[pl.ds(q0, tq), e * LANES:(e + 1) * LANES]
            dst_ref[e] = lax.dot_general(k, q, NT_DIMS, preferred_element_type=F32)

    def update(n, src_ref, diag):
        for e in range(2):
            st = src_ref[e]
            if diag is not None:
                st = jnp.where(kv_i + diag * ts <= q_i, st, -jnp.inf)
            m_prev = m_ref[e]
            m_new = jnp.maximum(m_prev, jnp.max(st, axis=0, keepdims=True))
            pt = jnp.exp2(st - m_new).astype(BF16)
            alpha = jnp.exp2(m_prev - m_new)
            vt = vt_ref[0, n, e * hr:(e + 1) * hr, :]
            acc_ref[e] = alpha * acc_ref[e] + jnp.dot(vt, pt, preferred_element_type=F32)
            m_ref[e] = m_new

    bufs = (sa_ref, sb_ref)

    @pl.when(i == 0)
    def _():
        scores(0, 0, sa_ref)
        scores(0, 1, sb_ref)

    def steps(first, count):
        for u in range(count):
            update(first + u, bufs[u % 2], None)
            scores(i, first + u + 2, bufs[u % 2])

    first = 0
    for width in (16, 8, 4, 2):
        trips = (2 * i - first) // width

        def body(j, carry, first=first, width=width):
            steps(first + width * j, width)
            return carry

        lax.fori_loop(0, trips, body, 0)
        first = first + width * trips
    nxt = jnp.minimum(i + 1, pl.num_programs(2) - 1)
    update(2 * i, sa_ref, 0)
    scores(nxt, 0, sa_ref)
    update(2 * i + 1, sb_ref, 1)
    scores(nxt, 1, sb_ref)
    outs = []
    for e in range(2):
        acc = acc_ref[e]
        outs.append(acc[:head_dim, :] * (1.0 / acc[head_dim:head_dim + 1, :]))
    o_ref[...] = jnp.concatenate(outs, axis=0).T.astype(o_ref.dtype)


def _attn_prompt(qe, ke, vtb, *, batch, seq_len, n_heads, head_dim, tq):
    nsl, vrows, ts = vtb.shape[1:]
    hr = vrows // n_heads
    assert seq_len % tq == 0 and tq == 2 * ts and nsl * ts == seq_len
    nq = seq_len // tq
    return pl.pallas_call(
        functools.partial(_attn_prompt_kernel, head_dim=head_dim),
        grid=(batch, n_heads // 2, nq),
        in_specs=[pl.BlockSpec((seq_len, 2 * LANES), lambda b, p, i: (b, p)),
                  pl.BlockSpec((seq_len, 2 * LANES), lambda b, p, i: (b, p)),
                  pl.BlockSpec((1, nsl, 2 * hr, ts), lambda b, p, i: (b, 0, p, 0))],
        out_specs=pl.BlockSpec((tq, LANES), lambda b, p, i: (b * nq + i, p)),
        out_shape=jax.ShapeDtypeStruct((batch * seq_len, n_heads * head_dim), BF16),
        scratch_shapes=[pltpu.VMEM((2, ts, tq), F32), pltpu.VMEM((2, ts, tq), F32),
                        pltpu.VMEM((2, 1, tq), F32), pltpu.VMEM((2, hr, tq), F32)],
        compiler_params=_params(3), name="attn_prompt")(qe, ke, vtb)


def _attn_paged_kernel(pt_ref, qs_ref, cnrow_ref, cnrep_ref, kn_ref, vn_ref, lt_ref, *rest,
                       pages, n_heads, head_dim):
    k_refs = rest[:pages]
    v_refs = rest[pages:2 * pages]
    lf_refs = rest[2 * pages:3 * pages]
    o_ref, qbd_ref, m_ref, l_ref, acc_ref, carry_ref = rest[3 * pages:]
    del pt_ref
    g = pl.program_id(1)
    t_new, d = kn_ref.shape[1], kn_ref.shape[2]
    rows = n_heads * t_new
    page = k_refs[0].shape[2]

    def online_update(s, v_bf16, v_transposed):
        m_prev = m_ref[...]
        m_new = jnp.maximum(m_prev, jnp.max(s, axis=1, keepdims=True))
        p = jnp.exp(s - m_new)
        alpha = jnp.exp(m_prev - m_new)
        l_ref[...] = alpha * l_ref[...] + jnp.sum(p, axis=1, keepdims=True)
        if v_transposed:
            pv = lax.dot_general(p.astype(BF16), v_bf16, NT_DIMS, preferred_element_type=F32)
        else:
            pv = jnp.dot(p.astype(BF16), v_bf16, preferred_element_type=F32)
        acc_ref[...] = alpha * acc_ref[...] + pv
        m_ref[...] = m_new

    @pl.when(g == 0)
    def _():
        qt = jnp.concatenate([qs_ref[0]] * n_heads, axis=0)
        r_i = lax.broadcasted_iota(jnp.int32, (rows, d), 0)
        l_i = lax.broadcasted_iota(jnp.int32, (rows, d), 1)
        qbd = jnp.where(r_i // t_new == l_i // head_dim, qt, 0.0).astype(BF16)
        qbd_ref[...] = qbd
        m_ref[...] = jnp.full(m_ref.shape, -jnp.inf, F32)
        l_ref[...] = jnp.zeros_like(l_ref)
        acc_ref[...] = jnp.zeros_like(acc_ref)
        carry_ref[...] = jnp.zeros_like(carry_ref)
        pad = jnp.zeros((LANES - t_new, d), F32)
        kn = jnp.concatenate([kn_ref[0], pad], axis=0).astype(BF16)
        vn = jnp.concatenate([vn_ref[0], pad], axis=0).astype(BF16)
        s = lax.dot_general(qbd, kn, NT_DIMS, preferred_element_type=F32)
        s = s + cnrow_ref[0] - cnrep_ref[0]
        r2 = lax.broadcasted_iota(jnp.int32, s.shape, 0)
        c2 = lax.broadcasted_iota(jnp.int32, s.shape, 1)
        s = jnp.where(c2 <= r2 % t_new, s, -jnp.inf)
        online_update(s, vn, False)

    lf_all = jnp.concatenate([lf_refs[p][0] for p in range(pages)], axis=0)
    hi, mid, lo = _split3(lf_all)
    within = jnp.dot(jnp.concatenate([hi, mid, lo], axis=0), lt_ref[...], preferred_element_type=F32)
    within = within[:pages * n_heads] + within[pages * n_heads:2 * pages * n_heads] + within[2 * pages * n_heads:]
    totals = jnp.sum(lf_all, axis=1, keepdims=True)
    run = carry_ref[...]
    sufs = [None] * pages
    for p in range(pages - 1, -1, -1):
        rows_p = slice(p * n_heads, (p + 1) * n_heads)
        sufs[p] = within[rows_p, :] + run
        run = run + totals[rows_p, :]
    carry_ref[...] = run
    suf = jnp.concatenate(sufs, axis=1)
    bias = jnp.concatenate([jnp.broadcast_to(suf[h:h + 1, :], (t_new, pages * page)) for h in range(n_heads)],
                           axis=0)
    half = max(pages // 2, 1)
    groups = [range(g0, min(g0 + half, pages)) for g0 in range(0, pages, half)]
    qbd = qbd_ref[...]
    s_parts = []
    for grp in groups:
        kt = jnp.concatenate([k_refs[p][0].astype(BF16) for p in grp], axis=1)
        cols = slice(grp[0] * page, (grp[-1] + 1) * page)
        s_parts.append(jnp.dot(qbd, kt, preferred_element_type=F32) + cnrow_ref[0] + bias[:, cols])
    for grp, s in zip(groups, s_parts):
        vt = jnp.concatenate([v_refs[p][0].astype(BF16) for p in grp], axis=1)
        online_update(s, vt, True)

    @pl.when(g == pl.num_programs(1) - 1)
    def _():
        full = acc_ref[...] / l_ref[...]
        out = jnp.zeros((t_new, d), F32)
        l_i = lax.broadcasted_iota(jnp.int32, (t_new, d), 1)
        for h in range(n_heads):
            out = jnp.where(l_i // head_dim == h, full[h * t_new:(h + 1) * t_new, :], out)
        o_ref[0] = out


def _attn_paged(page_table, qs, cn, k_new, v_new, cache_k, cache_v, cache_lf, *, n_heads, head_dim, pages):
    nseq, t_new, d = k_new.shape
    n_pool, page = cache_k.shape[0], cache_k.shape[1]
    n_pages = page_table.shape[1]
    rows = n_heads * t_new
    assert rows == LANES and page == LANES and n_pages % pages == 0
    steps = n_pages // pages
    cn_t = jnp.transpose(cn, (0, 2, 1))
    cnrow = cn_t.reshape(nseq, rows, 1)
    cnrep = jnp.repeat(cn_t, t_new, axis=1)
    cnrep = jnp.pad(cnrep, ((0, 0), (0, 0), (0, LANES - t_new)))
    lt = jnp.asarray(np.tril(np.ones((page, page), np.float32), -1), BF16)
    ck = jnp.transpose(cache_k, (0, 2, 3, 1)).reshape(n_pool, d, page)
    cv = jnp.transpose(cache_v, (0, 2, 3, 1)).reshape(n_pool, d, page)
    clf = jnp.transpose(cache_lf, (0, 2, 1))

    def seq_map(s, g, pt):
        return (s, 0, 0)

    def const_map(s, g, pt):
        return (0, 0)

    def page_map(i):
        def f(s, g, pt):
            return (pt[s, (steps - 1 - g) * pages + i], 0, 0)
        return f

    in_specs = [pl.BlockSpec((1, t_new, d), seq_map), pl.BlockSpec((1, rows, 1), seq_map),
                pl.BlockSpec((1, rows, LANES), seq_map), pl.BlockSpec((1, t_new, d), seq_map),
                pl.BlockSpec((1, t_new, d), seq_map), pl.BlockSpec((page, page), const_map)]
    in_specs += [pl.BlockSpec((1, d, page), page_map(i)) for i in range(pages)]
    in_specs += [pl.BlockSpec((1, d, page), page_map(i)) for i in range(pages)]
    in_specs += [pl.BlockSpec((1, n_heads, page), page_map(i)) for i in range(pages)]
    grid_spec = pltpu.PrefetchScalarGridSpec(
        num_scalar_prefetch=1, grid=(nseq, steps), in_specs=in_specs,
        out_specs=pl.BlockSpec((1, t_new, d), seq_map),
        scratch_shapes=[pltpu.VMEM((rows, d), BF16), pltpu.VMEM((rows, 1), F32), pltpu.VMEM((rows, 1), F32),
                        pltpu.VMEM((rows, d), F32), pltpu.VMEM((n_heads, 1), F32)])
    kern = functools.partial(_attn_paged_kernel, pages=pages, n_heads=n_heads, head_dim=head_dim)
    return pl.pallas_call(
        kern, grid_spec=grid_spec, out_shape=jax.ShapeDtypeStruct((nseq, t_new, d), F32),
        compiler_params=_params(2), name="attn_paged")(
            page_table, qs, cnrow, cnrep, k_new, v_new, lt,
            *([ck] * pages), *([cv] * pages), *([clf] * pages))


def _gated_kernel(*refs, kind, seq_len, final_norm, has_prev, has_pre, split):
    it = iter(refs)
    x_ref, g_ref = next(it), next(it)
    a_ref, wo_ref = (next(it), next(it)) if has_pre else (None, None)
    up_refs = [next(it) for _ in range(2 if kind == "ffn" else 3)]
    dw_ref, wd_ref = next(it), next(it)
    gfin_ref = next(it) if final_norm else None
    p0_ref, p1_ref = (next(it), next(it)) if has_prev else (None, None)
    o_ref, st_ref, h_ref, gate_ref, ext_ref = next(it), next(it), next(it), next(it), next(it)
    res_ref = next(it) if has_pre else x_ref
    acc_ref = next(it) if split else None
    tm = x_ref.shape[0]
    tf = up_refs[0].shape[1]
    r = pl.program_id(0)
    f = pl.program_id(1)
    pad = ext_ref.shape[1] - tm

    @pl.when(f == 0)
    def _():
        x_in = x_ref[...]
        if has_pre:
            x_in = x_in + jnp.dot(a_ref[...], wo_ref[...], preferred_element_type=F32)
            res_ref[...] = x_in
        h_ref[...] = _rmsnorm(x_in, g_ref[...]).astype(BF16)

    h = h_ref[...]
    ext = ext_ref.at[f]
    tiles_per_seq = 1 if has_prev else seq_len // tm

    @pl.when(r % tiles_per_seq == 0)
    def _():
        ext[0:pad, :] = jnp.zeros((pad, tf), F32)

    for c0 in range(0, tf, GATE_CHUNK):
        cols = slice(c0, min(c0 + GATE_CHUNK, tf))
        ups = [jnp.dot(h, w[:, cols], preferred_element_type=F32) for w in up_refs]
        u = ups[0] if kind == "ffn" else ups[1] * ups[2]
        ext[pad:pad + tm, cols] = u
        u1 = ext[pad - 1:pad - 1 + tm, cols]
        u2 = ext[pad - 2:pad - 2 + tm, cols]
        ext[0:pad, cols] = ext[tm:tm + pad, cols]
        if has_prev:
            t = lax.broadcasted_iota(jnp.int32, u.shape, 0) % seq_len
            p0, p1 = p0_ref[:, cols], p1_ref[:, cols]
            u1 = jnp.where(t == 0, p1, u1)
            u2 = jnp.where(t == 0, p0, jnp.where(t == 1, p1, u2))
            st_ref[:, cols] = u
        else:
            st_ref[0, :, cols] = u[tm - (CONV_W - 1):, :]
        conv = dw_ref[0:1, cols] * u2 + dw_ref[1:2, cols] * u1 + dw_ref[2:3, cols] * u
        if kind == "ffn":
            gated = conv * jax.nn.sigmoid(conv) * ups[1]
        else:
            gated = ups[0] * conv
        gate_ref[:, cols] = gated.astype(BF16)

    down = jnp.dot(gate_ref[...], wd_ref[...], preferred_element_type=F32)
    last = pl.num_programs(1) - 1
    if acc_ref is not None:
        @pl.when(f == 0)
        def _():
            acc_ref[...] = down

        @pl.when(jnp.logical_and(f > 0, f < last))
        def _():
            acc_ref[...] += down

    @pl.when(f == last)
    def _():
        y = res_ref[...] + down
        if acc_ref is not None:
            y = y + acc_ref[...]
        if final_norm:
            y = _rmsnorm(y, gfin_ref[...])
        o_ref[...] = y


def _gated(x2, g, ups, dw, wd, *, kind, seq_len, tm, tf, gfin=None, prev=None, pre=None):
    r, d = x2.shape
    fdim = ups[0].shape[1]
    assert r % tm == 0 and fdim % tf == 0
    has_prev = prev is not None
    if has_prev:
        assert tm == r and tm % seq_len == 0 and seq_len >= CONV_W - 1
    else:
        assert seq_len % tm == 0
    nf = fdim // tf
    n_up = len(ups)
    pad = 8
    row = lambda i, j: (i, 0)
    const = lambda i, j: (0, 0)
    once = dict(pipeline_mode=pl.Buffered(1)) if nf == 1 else {}
    in_specs = [pl.BlockSpec((tm, d), row), pl.BlockSpec((1, d), const)]
    args = [x2, g]
    if pre is not None:
        a, wo = pre
        in_specs += [pl.BlockSpec((tm, a.shape[1]), row),
                     pl.BlockSpec(wo.shape, const, pipeline_mode=pl.Buffered(1))]
        args += [a, wo]
    in_specs += [pl.BlockSpec((d, tf), lambda i, j: (0, j), **once) for _ in range(n_up)]
    in_specs += [pl.BlockSpec((CONV_W, tf), lambda i, j: (0, j)),
                 pl.BlockSpec((tf, d), lambda i, j: (j, 0), **once)]
    args += [*ups, dw, wd]
    if gfin is not None:
        in_specs.append(pl.BlockSpec((1, d), const))
        args.append(gfin)
    if has_prev:
        in_specs += [pl.BlockSpec((tm, tf), lambda i, j: (i, j))] * 2
        args += list(prev)
        st_shape = jax.ShapeDtypeStruct((r, fdim), F32)
        st_spec = pl.BlockSpec((tm, tf), lambda i, j: (i, j))
    else:
        st_shape = jax.ShapeDtypeStruct((r // tm, CONV_W - 1, fdim), F32)
        st_spec = pl.BlockSpec((1, CONV_W - 1, tf), lambda i, j: (i, 0, j))
    kern = functools.partial(_gated_kernel, kind=kind, seq_len=seq_len, final_norm=gfin is not None,
                             has_prev=has_prev, has_pre=pre is not None, split=nf > 1)
    scratch = [pltpu.VMEM((tm, d), BF16), pltpu.VMEM((tm, tf), BF16), pltpu.VMEM((nf, tm + pad, tf), F32)]
    scratch += [pltpu.VMEM((tm, d), F32)] * ((pre is not None) + (nf > 1))
    return pl.pallas_call(
        kern, grid=(r // tm, nf), in_specs=in_specs,
        out_specs=[pl.BlockSpec((tm, d), row), st_spec],
        out_shape=[jax.ShapeDtypeStruct((r, d), F32), st_shape],
        scratch_shapes=scratch,
        compiler_params=_params(2), name="gated_" + kind + ("_dec" if has_prev else ""))(*args)


def _tiles(seq_len, d_ff, d_model):
    def largest_divisor(n, cap, unit):
        best = unit
        for t in range(unit, min(n, cap) + 1, unit):
            if n % t == 0:
                best = t
        return best
    return dict(proj=largest_divisor(seq_len, 256, 8), attn=largest_divisor(seq_len, 512, 2 * LANES),
                rows=largest_divisor(seq_len, 512, 8),
                ff=d_ff, conv=d_model)


def _trunk(x3, *, paged, conv_prev, ffn_prev, w, cache, n_heads):
    b, t, d = x3.shape
    head_dim = d // n_heads
    r = b * t
    x2 = x3.reshape(r, d)
    d_ff = w["ffn_a"][0].shape[1]
    tl = _tiles(t, d_ff, d)
    if paged:
        tl = dict(tl, proj=r, rows=r)
    depth = len(w["ffn_a"])
    ks, vs, lfs, convs, ffns = [], [], [], [], []
    y = None
    for layer in range(depth):
        gm = w["norm_mix"][layer][None, :]
        if layer % 2 == 0:
            a = layer // 2
            if paged:
                k, v, lf, cn, qs = _proj(x2, gm, w["qkv"][a], w["wf"][a], w["bf"][a], seq_len=t,
                                         n_heads=n_heads, head_dim=head_dim, tm=tl["proj"])
                o = _attn_paged(cache["page_table"], qs.reshape(b, t, d), cn[:, :n_heads].reshape(b, t, n_heads),
                                k.reshape(b, t, d), v.reshape(b, t, d), cache["k"][a], cache["v"][a],
                                cache["lf"][a], n_heads=n_heads, head_dim=head_dim, pages=cache["pages"])
                o = o.reshape(r, d).astype(BF16)
                ks.append(k.reshape(b, t, n_heads, head_dim))
                vs.append(v.reshape(b, t, n_heads, head_dim))
            else:
                kt, vt, lf, qe, ke, vtb = _proj(x2, gm, w["qkv"][a], w["wf"][a], w["bf"][a], seq_len=t,
                                                n_heads=n_heads, head_dim=head_dim, tm=tl["proj"],
                                                tk=tl["attn"] // 2)
                o = _attn_prompt(qe, ke, vtb, batch=b, seq_len=t, n_heads=n_heads, head_dim=head_dim,
                                 tq=tl["attn"])
                ks.append(jnp.transpose(kt.reshape(b, n_heads, head_dim, t), (0, 3, 1, 2)))
                vs.append(jnp.transpose(vt.reshape(b, n_heads, head_dim, t), (0, 3, 1, 2)))
            pre = (o, w["wo"][a])
            lfs.append(lf[:, :n_heads].reshape(b, t, n_heads))
        else:
            pre = None
            c = layer // 2
            ups = [w["conv_in"][c][:, j * d:(j + 1) * d] for j in range(3)]
            prev = None
            if paged:
                prev = tuple(jnp.repeat(conv_prev[c][:, j], t, axis=0) for j in range(CONV_W - 1))
            x2, st = _gated(x2, gm, ups, w["conv_dw"][c], w["conv_out"][c], kind="conv", seq_len=t,
                            tm=tl["rows"], tf=tl["conv"], prev=prev)
            convs.append(_state_rows(st, b, t, paged, tl["rows"]))
        gf = w["norm_ffn"][layer][None, :]
        prev = None
        if paged:
            prev = tuple(jnp.repeat(ffn_prev[layer][:, j], t, axis=0) for j in range(CONV_W - 1))
        gfin = w["norm_final"][None, :] if layer == depth - 1 else None
        x2, st = _gated(x2, gf, [w["ffn_a"][layer], w["ffn_b"][layer]], w["ffn_dw"][layer],
                        w["ffn_down"][layer], kind="ffn", seq_len=t, tm=tl["rows"], tf=tl["ff"],
                        gfin=gfin, prev=prev, pre=pre)
        ffns.append(_state_rows(st, b, t, paged, tl["rows"]))
        y = x2
    return (y.reshape(b, t, d), jnp.stack(ks), jnp.stack(vs), jnp.stack(lfs), jnp.stack(convs),
            jnp.stack(ffns))


def _state_rows(st, b, t, paged, tm):
    if paged:
        return st.reshape(b, t, -1)[:, t - (CONV_W - 1):]
    tiles_per_seq = t // tm
    return st[tiles_per_seq - 1::tiles_per_seq]


def kernel(x_prompt, x_sample, cache_k, cache_v, cache_logf, state_conv, state_ffn, page_table, norm_mix, norm_ffn, norm_final, attn_w_qkv, attn_w_f, attn_b_f, attn_w_o, conv_w_in, conv_w_dw, conv_w_out, ffn_w_a, ffn_w_b, ffn_w_dw, ffn_w_down):
    n_attn, d, n_heads = attn_w_f.shape
    n_conv = conv_w_in.shape[0]
    depth = ffn_w_a.shape[0]
    wf = jnp.pad(attn_w_f, ((0, 0), (0, 0), (0, LANES - n_heads))).astype(BF16)
    bf = jnp.pad(attn_b_f, ((0, 0), (0, LANES - n_heads)))[:, None, :]
    w = dict(
        norm_mix=norm_mix, norm_ffn=norm_ffn, norm_final=norm_final,
        qkv=[attn_w_qkv[a].astype(BF16) for a in range(n_attn)],
        wf=[wf[a] for a in range(n_attn)], bf=[bf[a] for a in range(n_attn)],
        wo=[attn_w_o[a].astype(BF16) for a in range(n_attn)],
        conv_in=[conv_w_in[c].astype(BF16) for c in range(n_conv)],
        conv_dw=[conv_w_dw[c] for c in range(n_conv)],
        conv_out=[conv_w_out[c].astype(BF16) for c in range(n_conv)],
        ffn_a=[ffn_w_a[l].astype(BF16) for l in range(depth)],
        ffn_b=[ffn_w_b[l].astype(BF16) for l in range(depth)],
        ffn_dw=[ffn_w_dw[l] for l in range(depth)],
        ffn_down=[ffn_w_down[l].astype(BF16) for l in range(depth)])
    n_pages = page_table.shape[1]
    pages = 8 if n_pages % 8 == 0 else 1
    cache = dict(k=cache_k, v=cache_v, lf=cache_logf, page_table=page_table, pages=pages)
    prompt = _trunk(x_prompt, paged=False, conv_prev=None, ffn_prev=None, w=w, cache=None, n_heads=n_heads)
    sample = _trunk(x_sample, paged=True, conv_prev=state_conv, ffn_prev=state_ffn, w=w, cache=cache,
                    n_heads=n_heads)
    y_p, k_p, v_p, lf_p, conv_p, ffn_p = prompt
    y_s, k_s, v_s, lf_s, conv_s, ffn_s = sample
    return (y_p, y_s, k_p, v_p, lf_p, conv_p, ffn_p, k_s, v_s, lf_s, conv_s, ffn_s)
```

```python
import functools

import numpy as np
import jax
import jax.numpy as jnp
from jax import lax
from jax.experimental import pallas as pl
from jax.experimental.pallas import tpu as pltpu

F32 = jnp.float32
BF16 = jnp.bfloat16
NORM_EPS = 1e-6
CONV_W = 3
LANES = 128
V7X_VMEM_LIMIT = 56 * 1024 * 1024
NT_DIMS = (((1,), (1,)), ((), ()))
LOG2E = 1.4426950408889634
ONES_ROWS = 16
GATE_CHUNK = 256


def _params(n_axes):
    return pltpu.CompilerParams(dimension_semantics=("arbitrary",) * n_axes,
                                vmem_limit_bytes=V7X_VMEM_LIMIT)


def _rmsnorm(x, g):
    ms = jnp.mean(x * x, axis=-1, keepdims=True)
    return x * lax.rsqrt(ms + NORM_EPS) * g


def _log_sigmoid(z):
    return jnp.minimum(z, 0.0) - jnp.log1p(jnp.exp(-jnp.abs(z)))


def _split3(x):
    hi = x.astype(BF16)
    r1 = x - hi.astype(F32)
    mid = r1.astype(BF16)
    lo = (r1 - mid.astype(F32)).astype(BF16)
    return hi, mid, lo


def _dot3(sel, x, *, nt=False, sel_right=False):
    out = None
    for part in _split3(x):
        if sel_right:
            t = jnp.dot(part, sel, preferred_element_type=F32)
        elif nt:
            t = lax.dot_general(sel, part, NT_DIMS, preferred_element_type=F32)
        else:
            t = jnp.dot(sel, part, preferred_element_type=F32)
        out = t if out is None else out + t
    return out


def _proj_kernel(x_ref, g_ref, w_ref, wf_ref, bf_ref, tri_ref, *rest,
                 tiles_per_seq, n_heads, head_dim, ext):
    if ext:
        wvt_ref, selq_ref, selk_ref, kt_ref, vt_ref, lf_ref, qe_ref, ke_ref, vtb_ref, carry_ref = rest
    else:
        k_ref, v_ref, lf_ref, c_ref, qs_ref, carry_ref = rest
    tm, d = x_ref.shape
    i = pl.program_id(0)

    @pl.when(i % tiles_per_seq == 0)
    def _():
        carry_ref[...] = jnp.zeros_like(carry_ref)

    h = _rmsnorm(x_ref[...], g_ref[...]).astype(BF16)
    proj = jnp.dot(h, w_ref[...], preferred_element_type=F32)
    k = proj[:, d:2 * d]

    z = jnp.dot(h, wf_ref[...], preferred_element_type=F32) + bf_ref[...]
    lane = lax.broadcasted_iota(jnp.int32, z.shape, 1)
    lf = jnp.where(lane < n_heads, _log_sigmoid(z), 0.0)
    lf_ref[...] = lf

    c = _dot3(tri_ref[...], lf) + carry_ref[...]
    carry_ref[...] = c[tm - 1:tm, :]

    if not ext:
        k_ref[...] = k
        v_ref[...] = proj[:, 2 * d:]
        c_ref[...] = c
        qs_ref[...] = proj[:, :d] * (head_dim ** -0.5)
        return

    kt_ref[0] = k.T
    vt = lax.dot_general(wvt_ref[...], h, NT_DIMS, preferred_element_type=F32)
    vt_ref[0] = vt
    ones_rows = jnp.ones((ONES_ROWS, tm), F32)
    pieces = []
    for hd in range(n_heads):
        pieces += [vt[hd * head_dim:(hd + 1) * head_dim, :], ones_rows]
    vtb_ref[0, 0] = jnp.concatenate(pieces, axis=0).astype(BF16)

    q = proj[:, :d] * (head_dim ** -0.5 * LOG2E)
    c_hi, c_mid, c_lo = (p.astype(F32) for p in _split3(c * LOG2E))
    ones = (lane == 3 * n_heads).astype(F32)
    c3 = (c_hi + pltpu.roll(c_mid, n_heads, 1) + pltpu.roll(c_lo, 2 * n_heads, 1) + ones).astype(BF16)
    bq = jnp.dot(c3, selq_ref[...], preferred_element_type=F32)
    bk = jnp.dot(c3, selk_ref[...], preferred_element_type=F32)
    lane2 = lax.broadcasted_iota(jnp.int32, (1, bq.shape[1]), 1)
    own = (((lane2 // head_dim) + 1) & 2) == 0

    def dup(a):
        return jnp.concatenate([a[:, (j // 2) * LANES:(j // 2 + 1) * LANES] for j in range(n_heads)], axis=1)

    qe_ref[...] = jnp.where(own, dup(q), bq).astype(BF16)
    ke_ref[...] = jnp.where(own, dup(k), bk).astype(BF16)


def _bias_selectors(n_heads, head_dim):
    selq = np.zeros((LANES, n_heads * LANES), np.float32)
    selk = np.zeros((LANES, n_heads * LANES), np.float32)
    one = 3 * n_heads
    for h in range(n_heads):
        base = h * LANES + (head_dim if h % 2 == 0 else 0)
        for part in range(3):
            selq[part * n_heads + h, base + 2 * part] = 1.0
            selq[one, base + 2 * part + 1] = 1.0
            selk[one, base + 2 * part] = 1.0
            selk[part * n_heads + h, base + 2 * part + 1] = -1.0
    return jnp.asarray(selq, BF16), jnp.asarray(selk, BF16)


def _proj(x2, g, wqkv, wf, bf, *, seq_len, n_heads, head_dim, tm, tk=None):
    r, d = x2.shape
    ext = tk is not None
    assert 2 * head_dim == LANES and n_heads % 2 == 0 and 3 * n_heads < LANES
    assert r % tm == 0
    if seq_len >= tm:
        assert seq_len % tm == 0
        tiles_per_seq = seq_len // tm
        tri = np.tril(np.ones((tm, tm), np.float32))
    else:
        assert tm % seq_len == 0
        tiles_per_seq = 1
        idx = np.arange(tm)
        tri = ((idx[:, None] >= idx[None, :]) & (idx[:, None] // seq_len == idx[None, :] // seq_len))
        tri = tri.astype(np.float32)
    tri = jnp.asarray(tri, BF16)
    row = lambda i: (i, 0)
    const = lambda i: (0, 0)
    lf_shape, lf_spec = jax.ShapeDtypeStruct((r, LANES), F32), pl.BlockSpec((tm, LANES), row)
    if ext:
        assert seq_len % tk == 0 and tk % tm == 0
        batch, per_blk = r // seq_len, tk // tm
        selq, selk = _bias_selectors(n_heads, head_dim)
        w, wvt = wqkv[:, :2 * d], wqkv[:, 2 * d:].T
        in_specs = [pl.BlockSpec((tm, d), row), pl.BlockSpec((1, d), const),
                    pl.BlockSpec((d, 2 * d), const), pl.BlockSpec((d, LANES), const),
                    pl.BlockSpec((1, LANES), const), pl.BlockSpec((tm, tm), const),
                    pl.BlockSpec((d, d), const)] + [pl.BlockSpec((LANES, n_heads * LANES), const)] * 2
        args = [x2, g, w, wf, bf, tri, wvt, selq, selk]
        t_spec = pl.BlockSpec((1, d, tm), lambda i: (i // tiles_per_seq, 0, i % tiles_per_seq))
        out_shape = [jax.ShapeDtypeStruct((batch, d, seq_len), F32), jax.ShapeDtypeStruct((batch, d, seq_len), F32),
                     lf_shape,
                     jax.ShapeDtypeStruct((r, n_heads * LANES), BF16),
                     jax.ShapeDtypeStruct((r, n_heads * LANES), BF16),
                     jax.ShapeDtypeStruct((batch, seq_len // tk, n_heads * (head_dim + ONES_ROWS), tk), BF16)]
        out_specs = [t_spec, t_spec, lf_spec,
                     pl.BlockSpec((tm, n_heads * LANES), row), pl.BlockSpec((tm, n_heads * LANES), row),
                     pl.BlockSpec((1, 1, n_heads * (head_dim + ONES_ROWS), tm),
                                  lambda i: (i // tiles_per_seq, (i % tiles_per_seq) // per_blk, 0, i % per_blk))]
    else:
        in_specs = [pl.BlockSpec((tm, d), row), pl.BlockSpec((1, d), const),
                    pl.BlockSpec((d, 3 * d), const), pl.BlockSpec((d, LANES), const),
                    pl.BlockSpec((1, LANES), const), pl.BlockSpec((tm, tm), const)]
        args = [x2, g, wqkv, wf, bf, tri]
        out_shape = [jax.ShapeDtypeStruct((r, d), F32), jax.ShapeDtypeStruct((r, d), F32), lf_shape,
                     jax.ShapeDtypeStruct((r, LANES), F32), jax.ShapeDtypeStruct((r, d), F32)]
        out_specs = [pl.BlockSpec((tm, d), row), pl.BlockSpec((tm, d), row), lf_spec,
                     pl.BlockSpec((tm, LANES), row), pl.BlockSpec((tm, d), row)]
    kern = functools.partial(_proj_kernel, tiles_per_seq=tiles_per_seq, n_heads=n_heads,
                             head_dim=head_dim, ext=ext)
    return pl.pallas_call(
        kern, grid=(r // tm,), in_specs=in_specs, out_specs=out_specs, out_shape=out_shape,
        scratch_shapes=[pltpu.VMEM((1, LANES), F32)],
        compiler_params=_params(1), name="proj_ext" if ext else "proj_dec")(*args)


def _attn_prompt_kernel(qe_ref, ke_ref, vt_ref, o_ref, s0_ref, s1_ref, s2_ref, s3_ref, m_ref, acc_ref, *,
                        head_dim):
    bufs = (s0_ref, s1_ref, s2_ref, s3_ref)
    tq = o_ref.shape[0]
    ts = s0_ref.shape[1]
    hr = acc_ref.shape[1]
    i = pl.program_id(2)
    kv_i = lax.broadcasted_iota(jnp.int32, (ts, tq), 0)
    q_i = lax.broadcasted_iota(jnp.int32, (ts, tq), 1)
    m_ref[...] = jnp.full(m_ref.shape, -jnp.inf, F32)
    acc_ref[...] = jnp.zeros_like(acc_ref)

    def scores(qi, n, dst_ref):
        k0 = pl.multiple_of(n * ts, ts)
        q0 = pl.multiple_of(qi * tq, tq)
        for e in range(2):
            k = ke_ref[pl.ds(k0, ts), e * LANES:(e + 1) * LANES]
            q = qe_ref[pl.ds(q0, tq), e * LANES:(e + 1) * LANES]
            dst_ref[e] = lax.dot_general(k, q, NT_DIMS, preferred_element_type=F32)

    def update(n, src_ref, diag):
        for e in range(2):
            st = src_ref[e]
            if diag is not None:
                st = jnp.where(kv_i + diag * ts <= q_i, st, -jnp.inf)
            m_prev = m_ref[e]
            m_new = jnp.maximum(m_prev, jnp.max(st, axis=0, keepdims=True))
            pt = jnp.exp2(st - m_new).astype(BF16)
            alpha = jnp.exp2(m_prev - m_new)
            vt = vt_ref[0, n, e * hr:(e + 1) * hr, :]
            acc_ref[e] = alpha * acc_ref[e] + jnp.dot(vt, pt, preferred_element_type=F32)
            m_ref[e] = m_new

    n_buf = len(bufs)

    @pl.when(i == 0)
    def _():
        scores(0, 0, bufs[0])
        scores(0, 1, bufs[1])

    def steps(first, count):
        for u in range(count):
            scores(i, first + u + 2, bufs[(u + 2) % n_buf])
            update(first + u, bufs[u % n_buf], None)

    first = 0
    for width in (16, 8, 4, 2):
        trips = (2 * i - first) // width

        def body(j, carry, first=first, width=width):
            steps(first + width * j, width)
            return carry

        lax.fori_loop(0, trips, body, 0)
        first = first + width * trips
    nxt = jnp.minimum(i + 1, pl.num_programs(2) - 1)

    @pl.when(i % 2 == 0)
    def _():
        update(2 * i, bufs[0], 0)
        scores(nxt, 0, bufs[0])
        update(2 * i + 1, bufs[1], 1)
        scores(nxt, 1, bufs[1])

    @pl.when(i % 2 == 1)
    def _():
        scores(nxt, 0, bufs[0])
        update(2 * i, bufs[2], 0)
        scores(nxt, 1, bufs[1])
        update(2 * i + 1, bufs[3], 1)

    outs = []
    for e in range(2):
        acc = acc_ref[e]
        outs.append(acc[:head_dim, :] * (1.0 / acc[head_dim:head_dim + 1, :]))
    o_ref[...] = jnp.concatenate(outs, axis=0).T.astype(o_ref.dtype)


def _attn_prompt(qe, ke, vtb, *, batch, seq_len, n_heads, head_dim, tq):
    nsl, vrows, ts = vtb.shape[1:]
    hr = vrows // n_heads
    assert seq_len % tq == 0 and tq == 2 * ts and nsl * ts == seq_len
    nq = seq_len // tq
    return pl.pallas_call(
        functools.partial(_attn_prompt_kernel, head_dim=head_dim),
        grid=(batch, n_heads // 2, nq),
        in_specs=[pl.BlockSpec((seq_len, 2 * LANES), lambda b, p, i: (b, p)),
                  pl.BlockSpec((seq_len, 2 * LANES), lambda b, p, i: (b, p)),
                  pl.BlockSpec((1, nsl, 2 * hr, ts), lambda b, p, i: (b, 0, p, 0))],
        out_specs=pl.BlockSpec((tq, LANES), lambda b, p, i: (b * nq + i, p)),
        out_shape=jax.ShapeDtypeStruct((batch * seq_len, n_heads * head_dim), BF16),
        scratch_shapes=[pltpu.VMEM((2, ts, tq), F32)] * 4 + [pltpu.VMEM((2, 1, tq), F32),
                                                            pltpu.VMEM((2, hr, tq), F32)],
        compiler_params=_params(3), name="attn_prompt")(qe, ke, vtb)


def _attn_paged_kernel(pt_ref, qs_ref, cnrow_ref, cnrep_ref, kn_ref, vn_ref, lt_ref, *rest,
                       pages, n_heads, head_dim):
    k_refs = rest[:pages]
    v_refs = rest[pages:2 * pages]
    lf_refs = rest[2 * pages:3 * pages]
    o_ref, qbd_ref, m_ref, l_ref, acc_ref, carry_ref = rest[3 * pages:]
    del pt_ref
    g = pl.program_id(1)
    t_new, d = kn_ref.shape[1], kn_ref.shape[2]
    rows = n_heads * t_new
    page = k_refs[0].shape[2]

    def online_update(s, v_bf16, v_transposed):
        m_prev = m_ref[...]
        m_new = jnp.maximum(m_prev, jnp.max(s, axis=1, keepdims=True))
        p = jnp.exp(s - m_new)
        alpha = jnp.exp(m_prev - m_new)
        l_ref[...] = alpha * l_ref[...] + jnp.sum(p, axis=1, keepdims=True)
        if v_transposed:
            pv = lax.dot_general(p.astype(BF16), v_bf16, NT_DIMS, preferred_element_type=F32)
        else:
            pv = jnp.dot(p.astype(BF16), v_bf16, preferred_element_type=F32)
        acc_ref[...] = alpha * acc_ref[...] + pv
        m_ref[...] = m_new

    @pl.when(g == 0)
    def _():
        qt = jnp.concatenate([qs_ref[0]] * n_heads, axis=0)
        r_i = lax.broadcasted_iota(jnp.int32, (rows, d), 0)
        l_i = lax.broadcasted_iota(jnp.int32, (rows, d), 1)
        qbd = jnp.where(r_i // t_new == l_i // head_dim, qt, 0.0).astype(BF16)
        qbd_ref[...] = qbd
        m_ref[...] = jnp.full(m_ref.shape, -jnp.inf, F32)
        l_ref[...] = jnp.zeros_like(l_ref)
        acc_ref[...] = jnp.zeros_like(acc_ref)
        carry_ref[...] = jnp.zeros_like(carry_ref)
        pad = jnp.zeros((LANES - t_new, d), F32)
        kn = jnp.concatenate([kn_ref[0], pad], axis=0).astype(BF16)
        vn = jnp.concatenate([vn_ref[0], pad], axis=0).astype(BF16)
        s = lax.dot_general(qbd, kn, NT_DIMS, preferred_element_type=F32)
        s = s + cnrow_ref[0] - cnrep_ref[0]
        r2 = lax.broadcasted_iota(jnp.int32, s.shape, 0)
        c2 = lax.broadcasted_iota(jnp.int32, s.shape, 1)
        s = jnp.where(c2 <= r2 % t_new, s, -jnp.inf)
        online_update(s, vn, False)

    lf_all = jnp.concatenate([lf_refs[p][0] for p in range(pages)], axis=0)
    hi, mid, lo = _split3(lf_all)
    within = jnp.dot(jnp.concatenate([hi, mid, lo], axis=0), lt_ref[...], preferred_element_type=F32)
    within = within[:pages * n_heads] + within[pages * n_heads:2 * pages * n_heads] + within[2 * pages * n_heads:]
    totals = jnp.sum(lf_all, axis=1, keepdims=True)
    run = carry_ref[...]
    sufs = [None] * pages
    for p in range(pages - 1, -1, -1):
        rows_p = slice(p * n_heads, (p + 1) * n_heads)
        sufs[p] = within[rows_p, :] + run
        run = run + totals[rows_p, :]
    carry_ref[...] = run
    suf = jnp.concatenate(sufs, axis=1)
    bias = jnp.concatenate([jnp.broadcast_to(suf[h:h + 1, :], (t_new, pages * page)) for h in range(n_heads)],
                           axis=0)
    half = max(pages // 2, 1)
    groups = [range(g0, min(g0 + half, pages)) for g0 in range(0, pages, half)]
    qbd = qbd_ref[...]
    s_parts = []
    for grp in groups:
        kt = jnp.concatenate([k_refs[p][0].astype(BF16) for p in grp], axis=1)
        cols = slice(grp[0] * page, (grp[-1] + 1) * page)
        s_parts.append(jnp.dot(qbd, kt, preferred_element_type=F32) + cnrow_ref[0] + bias[:, cols])
    for grp, s in zip(groups, s_parts):
        vt = jnp.concatenate([v_refs[p][0].astype(BF16) for p in grp], axis=1)
        online_update(s, vt, True)

    @pl.when(g == pl.num_programs(1) - 1)
    def _():
        full = acc_ref[...] / l_ref[...]
        out = jnp.zeros((t_new, d), F32)
        l_i = lax.broadcasted_iota(jnp.int32, (t_new, d), 1)
        for h in range(n_heads):
            out = jnp.where(l_i // head_dim == h, full[h * t_new:(h + 1) * t_new, :], out)
        o_ref[0] = out


def _attn_paged(page_table, qs, cn, k_new, v_new, cache_k, cache_v, cache_lf, *, n_heads, head_dim, pages):
    nseq, t_new, d = k_new.shape
    n_pool, page = cache_k.shape[0], cache_k.shape[1]
    n_pages = page_table.shape[1]
    rows = n_heads * t_new
    assert rows == LANES and page == LANES and n_pages % pages == 0
    steps = n_pages // pages
    cn_t = jnp.transpose(cn, (0, 2, 1))
    cnrow = cn_t.reshape(nseq, rows, 1)
    cnrep = jnp.repeat(cn_t, t_new, axis=1)
    cnrep = jnp.pad(cnrep, ((0, 0), (0, 0), (0, LANES - t_new)))
    lt = jnp.asarray(np.tril(np.ones((page, page), np.float32), -1), BF16)
    ck = jnp.transpose(cache_k, (0, 2, 3, 1)).reshape(n_pool, d, page)
    cv = jnp.transpose(cache_v, (0, 2, 3, 1)).reshape(n_pool, d, page)
    clf = jnp.transpose(cache_lf, (0, 2, 1))

    def seq_map(s, g, pt):
        return (s, 0, 0)

    def const_map(s, g, pt):
        return (0, 0)

    def page_map(i):
        def f(s, g, pt):
            return (pt[s, (steps - 1 - g) * pages + i], 0, 0)
        return f

    in_specs = [pl.BlockSpec((1, t_new, d), seq_map), pl.BlockSpec((1, rows, 1), seq_map),
                pl.BlockSpec((1, rows, LANES), seq_map), pl.BlockSpec((1, t_new, d), seq_map),
                pl.BlockSpec((1, t_new, d), seq_map), pl.BlockSpec((page, page), const_map)]
    in_specs += [pl.BlockSpec((1, d, page), page_map(i)) for i in range(pages)]
    in_specs += [pl.BlockSpec((1, d, page), page_map(i)) for i in range(pages)]
    in_specs += [pl.BlockSpec((1, n_heads, page), page_map(i)) for i in range(pages)]
    grid_spec = pltpu.PrefetchScalarGridSpec(
        num_scalar_prefetch=1, grid=(nseq, steps), in_specs=in_specs,
        out_specs=pl.BlockSpec((1, t_new, d), seq_map),
        scratch_shapes=[pltpu.VMEM((rows, d), BF16), pltpu.VMEM((rows, 1), F32), pltpu.VMEM((rows, 1), F32),
                        pltpu.VMEM((rows, d), F32), pltpu.VMEM((n_heads, 1), F32)])
    kern = functools.partial(_attn_paged_kernel, pages=pages, n_heads=n_heads, head_dim=head_dim)
    return pl.pallas_call(
        kern, grid_spec=grid_spec, out_shape=jax.ShapeDtypeStruct((nseq, t_new, d), F32),
        compiler_params=_params(2), name="attn_paged")(
            page_table, qs, cnrow, cnrep, k_new, v_new, lt,
            *([ck] * pages), *([cv] * pages), *([clf] * pages))


def _gated_kernel(*refs, kind, seq_len, final_norm, has_prev, has_pre, split):
    it = iter(refs)
    x_ref, g_ref = next(it), next(it)
    a_ref, wo_ref = (next(it), next(it)) if has_pre else (None, None)
    up_refs = [next(it) for _ in range(2 if kind == "ffn" else 3)]
    dw_ref, wd_ref = next(it), next(it)
    gfin_ref = next(it) if final_norm else None
    p0_ref, p1_ref = (next(it), next(it)) if has_prev else (None, None)
    o_ref, st_ref, h_ref, gate_ref, ext_ref = next(it), next(it), next(it), next(it), next(it)
    res_ref = next(it) if has_pre else x_ref
    acc_ref = next(it) if split else None
    tm = x_ref.shape[0]
    tf = up_refs[0].shape[1]
    r = pl.program_id(0)
    f = pl.program_id(1)
    pad = ext_ref.shape[1] - tm

    @pl.when(f == 0)
    def _():
        x_in = x_ref[...]
        if has_pre:
            x_in = x_in + jnp.dot(a_ref[...], wo_ref[...], preferred_element_type=F32)
            res_ref[...] = x_in
        h_ref[...] = _rmsnorm(x_in, g_ref[...]).astype(BF16)

    h = h_ref[...]
    ext = ext_ref.at[f]
    tiles_per_seq = 1 if has_prev else seq_len // tm

    @pl.when(r % tiles_per_seq == 0)
    def _():
        ext[0:pad, :] = jnp.zeros((pad, tf), F32)

    for c0 in range(0, tf, GATE_CHUNK):
        cols = slice(c0, min(c0 + GATE_CHUNK, tf))
        ups = [jnp.dot(h, w[:, cols], preferred_element_type=F32) for w in up_refs]
        u = ups[0] if kind == "ffn" else ups[1] * ups[2]
        ext[pad:pad + tm, cols] = u
        u1 = ext[pad - 1:pad - 1 + tm, cols]
        u2 = ext[pad - 2:pad - 2 + tm, cols]
        ext[0:pad, cols] = ext[tm:tm + pad, cols]
        if has_prev:
            t = lax.broadcasted_iota(jnp.int32, u.shape, 0) % seq_len
            p0, p1 = p0_ref[:, cols], p1_ref[:, cols]
            u1 = jnp.where(t == 0, p1, u1)
            u2 = jnp.where(t == 0, p0, jnp.where(t == 1, p1, u2))
            st_ref[:, cols] = u
        else:
            st_ref[0, :, cols] = u[tm - (CONV_W - 1):, :]
        conv = dw_ref[0:1, cols] * u2 + dw_ref[1:2, cols] * u1 + dw_ref[2:3, cols] * u
        if kind == "ffn":
            gated = conv * jax.nn.sigmoid(conv) * ups[1]
        else:
            gated = ups[0] * conv
        gate_ref[:, cols] = gated.astype(BF16)

    down = jnp.dot(gate_ref[...], wd_ref[...], preferred_element_type=F32)
    last = pl.num_programs(1) - 1
    if acc_ref is not None:
        @pl.when(f == 0)
        def _():
            acc_ref[...] = down

        @pl.when(jnp.logical_and(f > 0, f < last))
        def _():
            acc_ref[...] += down

    @pl.when(f == last)
    def _():
        y = res_ref[...] + down
        if acc_ref is not None:
            y = y + acc_ref[...]
        if final_norm:
            y = _rmsnorm(y, gfin_ref[...])
        o_ref[...] = y


def _gated(x2, g, ups, dw, wd, *, kind, seq_len, tm, tf, gfin=None, prev=None, pre=None):
    r, d = x2.shape
    fdim = ups[0].shape[1]
    assert r % tm == 0 and fdim % tf == 0
    has_prev = prev is not None
    if has_prev:
        assert tm == r and tm % seq_len == 0 and seq_len >= CONV_W - 1
    else:
        assert seq_len % tm == 0
    nf = fdim // tf
    n_up = len(ups)
    pad = 8
    row = lambda i, j: (i, 0)
    const = lambda i, j: (0, 0)
    once = dict(pipeline_mode=pl.Buffered(1)) if nf == 1 else {}
    in_specs = [pl.BlockSpec((tm, d), row), pl.BlockSpec((1, d), const)]
    args = [x2, g]
    if pre is not None:
        a, wo = pre
        in_specs += [pl.BlockSpec((tm, a.shape[1]), row),
                     pl.BlockSpec(wo.shape, const, pipeline_mode=pl.Buffered(1))]
        args += [a, wo]
    in_specs += [pl.BlockSpec((d, tf), lambda i, j: (0, j), **once) for _ in range(n_up)]
    in_specs += [pl.BlockSpec((CONV_W, tf), lambda i, j: (0, j)),
                 pl.BlockSpec((tf, d), lambda i, j: (j, 0), **once)]
    args += [*ups, dw, wd]
    if gfin is not None:
        in_specs.append(pl.BlockSpec((1, d), const))
        args.append(gfin)
    if has_prev:
        in_specs += [pl.BlockSpec((tm, tf), lambda i, j: (i, j))] * 2
        args += list(prev)
        st_shape = jax.ShapeDtypeStruct((r, fdim), F32)
        st_spec = pl.BlockSpec((tm, tf), lambda i, j: (i, j))
    else:
        st_shape = jax.ShapeDtypeStruct((r // tm, CONV_W - 1, fdim), F32)
        st_spec = pl.BlockSpec((1, CONV_W - 1, tf), lambda i, j: (i, 0, j))
    kern = functools.partial(_gated_kernel, kind=kind, seq_len=seq_len, final_norm=gfin is not None,
                             has_prev=has_prev, has_pre=pre is not None, split=nf > 1)
    scratch = [pltpu.VMEM((tm, d), BF16), pltpu.VMEM((tm, tf), BF16), pltpu.VMEM((nf, tm + pad, tf), F32)]
    scratch += [pltpu.VMEM((tm, d), F32)] * ((pre is not None) + (nf > 1))
    return pl.pallas_call(
        kern, grid=(r // tm, nf), in_specs=in_specs,
        out_specs=[pl.BlockSpec((tm, d), row), st_spec],
        out_shape=[jax.ShapeDtypeStruct((r, d), F32), st_shape],
        scratch_shapes=scratch,
        compiler_params=_params(2), name="gated_" + kind + ("_dec" if has_prev else ""))(*args)


def _tiles(seq_len, d_ff, d_model):
    def largest_divisor(n, cap, unit):
        best = unit
        for t in range(unit, min(n, cap) + 1, unit):
            if n % t == 0:
                best = t
        return best
    return dict(proj=largest_divisor(seq_len, 256, 8), attn=largest_divisor(seq_len, 512, 2 * LANES),
                rows=largest_divisor(seq_len, 512, 8),
                ff=d_ff, conv=d_model)


def _trunk(x3, *, paged, conv_prev, ffn_prev, w, cache, n_heads):
    b, t, d = x3.shape
    head_dim = d // n_heads
    r = b * t
    x2 = x3.reshape(r, d)
    d_ff = w["ffn_a"][0].shape[1]
    tl = _tiles(t, d_ff, d)
    if paged:
        tl = dict(tl, proj=r, rows=r)
    depth = len(w["ffn_a"])
    ks, vs, lfs, convs, ffns = [], [], [], [], []
    y = None
    for layer in range(depth):
        gm = w["norm_mix"][layer][None, :]
        if layer % 2 == 0:
            a = layer // 2
            if paged:
                k, v, lf, cn, qs = _proj(x2, gm, w["qkv"][a], w["wf"][a], w["bf"][a], seq_len=t,
                                         n_heads=n_heads, head_dim=head_dim, tm=tl["proj"])
                o = _attn_paged(cache["page_table"], qs.reshape(b, t, d), cn[:, :n_heads].reshape(b, t, n_heads),
                                k.reshape(b, t, d), v.reshape(b, t, d), cache["k"][a], cache["v"][a],
                                cache["lf"][a], n_heads=n_heads, head_dim=head_dim, pages=cache["pages"])
                o = o.reshape(r, d).astype(BF16)
                ks.append(k.reshape(b, t, n_heads, head_dim))
                vs.append(v.reshape(b, t, n_heads, head_dim))
            else:
                kt, vt, lf, qe, ke, vtb = _proj(x2, gm, w["qkv"][a], w["wf"][a], w["bf"][a], seq_len=t,
                                                n_heads=n_heads, head_dim=head_dim, tm=tl["proj"],
                                                tk=tl["attn"] // 2)
                o = _attn_prompt(qe, ke, vtb, batch=b, seq_len=t, n_heads=n_heads, head_dim=head_dim,
                                 tq=tl["attn"])
                ks.append(jnp.transpose(kt.reshape(b, n_heads, head_dim, t), (0, 3, 1, 2)))
                vs.append(jnp.transpose(vt.reshape(b, n_heads, head_dim, t), (0, 3, 1, 2)))
            pre = (o, w["wo"][a])
            lfs.append(lf[:, :n_heads].reshape(b, t, n_heads))
        else:
            pre = None
            c = layer // 2
            ups = [w["conv_in"][c][:, j * d:(j + 1) * d] for j in range(3)]
            prev = None
            if paged:
                prev = tuple(jnp.repeat(conv_prev[c][:, j], t, axis=0) for j in range(CONV_W - 1))
            x2, st = _gated(x2, gm, ups, w["conv_dw"][c], w["conv_out"][c], kind="conv", seq_len=t,
                            tm=tl["rows"], tf=tl["conv"], prev=prev)
            convs.append(_state_rows(st, b, t, paged, tl["rows"]))
        gf = w["norm_ffn"][layer][None, :]
        prev = None
        if paged:
            prev = tuple(jnp.repeat(ffn_prev[layer][:, j], t, axis=0) for j in range(CONV_W - 1))
        gfin = w["norm_final"][None, :] if layer == depth - 1 else None
        x2, st = _gated(x2, gf, [w["ffn_a"][layer], w["ffn_b"][layer]], w["ffn_dw"][layer],
                        w["ffn_down"][layer], kind="ffn", seq_len=t, tm=tl["rows"], tf=tl["ff"],
                        gfin=gfin, prev=prev, pre=pre)
        ffns.append(_state_rows(st, b, t, paged, tl["rows"]))
        y = x2
    return (y.reshape(b, t, d), jnp.stack(ks), jnp.stack(vs), jnp.stack(lfs), jnp.stack(convs),
            jnp.stack(ffns))


def _state_rows(st, b, t, paged, tm):
    if paged:
        return st.reshape(b, t, -1)[:, t - (CONV_W - 1):]
    tiles_per_seq = t // tm
    return st[tiles_per_seq - 1::tiles_per_seq]


def kernel(x_prompt, x_sample, cache_k, cache_v, cache_logf, state_conv, state_ffn, page_table, norm_mix, norm_ffn, norm_final, attn_w_qkv, attn_w_f, attn_b_f, attn_w_o, conv_w_in, conv_w_dw, conv_w_out, ffn_w_a, ffn_w_b, ffn_w_dw, ffn_w_down):
    n_attn, d, n_heads = attn_w_f.shape
    n_conv = conv_w_in.shape[0]
    depth = ffn_w_a.shape[0]
    wf = jnp.pad(attn_w_f, ((0, 0), (0, 0), (0, LANES - n_heads))).astype(BF16)
    bf = jnp.pad(attn_b_f, ((0, 0), (0, LANES - n_heads)))[:, None, :]
    w = dict(
        norm_mix=norm_mix, norm_ffn=norm_ffn, norm_final=norm_final,
        qkv=[attn_w_qkv[a].astype(BF16) for a in range(n_attn)],
        wf=[wf[a] for a in range(n_attn)], bf=[bf[a] for a in range(n_attn)],
        wo=[attn_w_o[a].astype(BF16) for a in range(n_attn)],
        conv_in=[conv_w_in[c].astype(BF16) for c in range(n_conv)],
        conv_dw=[conv_w_dw[c] for c in range(n_conv)],
        conv_out=[conv_w_out[c].astype(BF16) for c in range(n_conv)],
        ffn_a=[ffn_w_a[l].astype(BF16) for l in range(depth)],
        ffn_b=[ffn_w_b[l].astype(BF16) for l in range(depth)],
        ffn_dw=[ffn_w_dw[l] for l in range(depth)],
        ffn_down=[ffn_w_down[l].astype(BF16) for l in range(depth)])
    n_pages = page_table.shape[1]
    pages = next(p for p in (16, 8, 4, 2, 1) if n_pages % p == 0)
    cache = dict(k=cache_k, v=cache_v, lf=cache_logf, page_table=page_table, pages=pages)
    prompt = _trunk(x_prompt, paged=False, conv_prev=None, ffn_prev=None, w=w, cache=None, n_heads=n_heads)
    sample = _trunk(x_sample, paged=True, conv_prev=state_conv, ffn_prev=state_ffn, w=w, cache=cache,
                    n_heads=n_heads)
    y_p, k_p, v_p, lf_p, conv_p, ffn_p = prompt
    y_s, k_s, v_s, lf_s, conv_s, ffn_s = sample
    return (y_p, y_s, k_p, v_p, lf_p, conv_p, ffn_p, k_s, v_s, lf_s, conv_s, ffn_s)
```

```python
import functools

import numpy as np
import jax
import jax.numpy as jnp
from jax import lax
from jax.experimental import pallas as pl
from jax.experimental.pallas import tpu as pltpu

F32 = jnp.float32
BF16 = jnp.bfloat16
NORM_EPS = 1e-6
CONV_W = 3
LANES = 128
V7X_VMEM_LIMIT = 56 * 1024 * 1024
NT_DIMS = (((1,), (1,)), ((), ()))
LOG2E = 1.4426950408889634
ONES_ROWS = 16
GATE_CHUNK = 256


def _params(n_axes):
    return pltpu.CompilerParams(dimension_semantics=("arbitrary",) * n_axes,
                                vmem_limit_bytes=V7X_VMEM_LIMIT)


def _rmsnorm(x, g):
    ms = jnp.mean(x * x, axis=-1, keepdims=True)
    return x * lax.rsqrt(ms + NORM_EPS) * g


def _log_sigmoid(z):
    return jnp.minimum(z, 0.0) - jnp.log1p(jnp.exp(-jnp.abs(z)))


def _split3(x):
    hi = x.astype(BF16)
    r1 = x - hi.astype(F32)
    mid = r1.astype(BF16)
    lo = (r1 - mid.astype(F32)).astype(BF16)
    return hi, mid, lo


def _dot3(sel, x, *, nt=False, sel_right=False):
    out = None
    for part in _split3(x):
        if sel_right:
            t = jnp.dot(part, sel, preferred_element_type=F32)
        elif nt:
            t = lax.dot_general(sel, part, NT_DIMS, preferred_element_type=F32)
        else:
            t = jnp.dot(sel, part, preferred_element_type=F32)
        out = t if out is None else out + t
    return out


def _proj_kernel(x_ref, g_ref, w_ref, wf_ref, bf_ref, tri_ref, *rest,
                 tiles_per_seq, n_heads, head_dim, ext):
    if ext:
        wvt_ref, selq_ref, selk_ref, kt_ref, vt_ref, lf_ref, qe_ref, ke_ref, vtb_ref, carry_ref = rest
    else:
        k_ref, v_ref, lf_ref, c_ref, qs_ref, carry_ref = rest
    tm, d = x_ref.shape
    i = pl.program_id(0)

    @pl.when(i % tiles_per_seq == 0)
    def _():
        carry_ref[...] = jnp.zeros_like(carry_ref)

    h = _rmsnorm(x_ref[...], g_ref[...]).astype(BF16)
    proj = jnp.dot(h, w_ref[...], preferred_element_type=F32)
    k = proj[:, d:2 * d]

    z = jnp.dot(h, wf_ref[...], preferred_element_type=F32) + bf_ref[...]
    lane = lax.broadcasted_iota(jnp.int32, z.shape, 1)
    lf = jnp.where(lane < n_heads, _log_sigmoid(z), 0.0)
    lf_ref[...] = lf

    c = _dot3(tri_ref[...], lf) + carry_ref[...]
    carry_ref[...] = c[tm - 1:tm, :]

    if not ext:
        k_ref[...] = k
        v_ref[...] = proj[:, 2 * d:]
        c_ref[...] = c
        qs_ref[...] = proj[:, :d] * (head_dim ** -0.5)
        return

    kt_ref[0] = k.T
    vt = lax.dot_general(wvt_ref[...], h, NT_DIMS, preferred_element_type=F32)
    vt_ref[0] = vt
    ones_rows = jnp.ones((ONES_ROWS, tm), F32)
    pieces = []
    for hd in range(n_heads):
        pieces += [vt[hd * head_dim:(hd + 1) * head_dim, :], ones_rows]
    vtb_ref[0, 0] = jnp.concatenate(pieces, axis=0).astype(BF16)

    q = proj[:, :d] * (head_dim ** -0.5 * LOG2E)
    c_hi, c_mid, c_lo = (p.astype(F32) for p in _split3(c * LOG2E))
    ones = (lane == 3 * n_heads).astype(F32)
    c3 = (c_hi + pltpu.roll(c_mid, n_heads, 1) + pltpu.roll(c_lo, 2 * n_heads, 1) + ones).astype(BF16)
    bq = jnp.dot(c3, selq_ref[...], preferred_element_type=F32)
    bk = jnp.dot(c3, selk_ref[...], preferred_element_type=F32)
    lane2 = lax.broadcasted_iota(jnp.int32, (1, bq.shape[1]), 1)
    own = (((lane2 // head_dim) + 1) & 2) == 0

    def dup(a):
        return jnp.concatenate([a[:, (j // 2) * LANES:(j // 2 + 1) * LANES] for j in range(n_heads)], axis=1)

    qe_ref[...] = jnp.where(own, dup(q), bq).astype(BF16)
    ke_ref[...] = jnp.where(own, dup(k), bk).astype(BF16)


def _bias_selectors(n_heads, head_dim):
    selq = np.zeros((LANES, n_heads * LANES), np.float32)
    selk = np.zeros((LANES, n_heads * LANES), np.float32)
    one = 3 * n_heads
    for h in range(n_heads):
        base = h * LANES + (head_dim if h % 2 == 0 else 0)
        for part in range(3):
            selq[part * n_heads + h, base + 2 * part] = 1.0
            selq[one, base + 2 * part + 1] = 1.0
            selk[one, base + 2 * part] = 1.0
            selk[part * n_heads + h, base + 2 * part + 1] = -1.0
    return jnp.asarray(selq, BF16), jnp.asarray(selk, BF16)


def _proj(x2, g, wqkv, wf, bf, *, seq_len, n_heads, head_dim, tm, tk=None):
    r, d = x2.shape
    ext = tk is not None
    assert 2 * head_dim == LANES and n_heads % 2 == 0 and 3 * n_heads < LANES
    assert r % tm == 0
    if seq_len >= tm:
        assert seq_len % tm == 0
        tiles_per_seq = seq_len // tm
        tri = np.tril(np.ones((tm, tm), np.float32))
    else:
        assert tm % seq_len == 0
        tiles_per_seq = 1
        idx = np.arange(tm)
        tri = ((idx[:, None] >= idx[None, :]) & (idx[:, None] // seq_len == idx[None, :] // seq_len))
        tri = tri.astype(np.float32)
    tri = jnp.asarray(tri, BF16)
    row = lambda i: (i, 0)
    const = lambda i: (0, 0)
    lf_shape, lf_spec = jax.ShapeDtypeStruct((r, LANES), F32), pl.BlockSpec((tm, LANES), row)
    if ext:
        assert seq_len % tk == 0 and tk % tm == 0
        batch, per_blk = r // seq_len, tk // tm
        selq, selk = _bias_selectors(n_heads, head_dim)
        w, wvt = wqkv[:, :2 * d], wqkv[:, 2 * d:].T
        in_specs = [pl.BlockSpec((tm, d), row), pl.BlockSpec((1, d), const),
                    pl.BlockSpec((d, 2 * d), const), pl.BlockSpec((d, LANES), const),
                    pl.BlockSpec((1, LANES), const), pl.BlockSpec((tm, tm), const),
                    pl.BlockSpec((d, d), const)] + [pl.BlockSpec((LANES, n_heads * LANES), const)] * 2
        args = [x2, g, w, wf, bf, tri, wvt, selq, selk]
        t_spec = pl.BlockSpec((1, d, tm), lambda i: (i // tiles_per_seq, 0, i % tiles_per_seq))
        out_shape = [jax.ShapeDtypeStruct((batch, d, seq_len), F32), jax.ShapeDtypeStruct((batch, d, seq_len), F32),
                     lf_shape,
                     jax.ShapeDtypeStruct((r, n_heads * LANES), BF16),
                     jax.ShapeDtypeStruct((r, n_heads * LANES), BF16),
                     jax.ShapeDtypeStruct((batch, seq_len // tk, n_heads * (head_dim + ONES_ROWS), tk), BF16)]
        out_specs = [t_spec, t_spec, lf_spec,
                     pl.BlockSpec((tm, n_heads * LANES), row), pl.BlockSpec((tm, n_heads * LANES), row),
                     pl.BlockSpec((1, 1, n_heads * (head_dim + ONES_ROWS), tm),
                                  lambda i: (i // tiles_per_seq, (i % tiles_per_seq) // per_blk, 0, i % per_blk))]
    else:
        in_specs = [pl.BlockSpec((tm, d), row), pl.BlockSpec((1, d), const),
                    pl.BlockSpec((d, 3 * d), const), pl.BlockSpec((d, LANES), const),
                    pl.BlockSpec((1, LANES), const), pl.BlockSpec((tm, tm), const)]
        args = [x2, g, wqkv, wf, bf, tri]
        out_shape = [jax.ShapeDtypeStruct((r, d), F32), jax.ShapeDtypeStruct((r, d), F32), lf_shape,
                     jax.ShapeDtypeStruct((r, LANES), F32), jax.ShapeDtypeStruct((r, d), F32)]
        out_specs = [pl.BlockSpec((tm, d), row), pl.BlockSpec((tm, d), row), lf_spec,
                     pl.BlockSpec((tm, LANES), row), pl.BlockSpec((tm, d), row)]
    kern = functools.partial(_proj_kernel, tiles_per_seq=tiles_per_seq, n_heads=n_heads,
                             head_dim=head_dim, ext=ext)
    return pl.pallas_call(
        kern, grid=(r // tm,), in_specs=in_specs, out_specs=out_specs, out_shape=out_shape,
        scratch_shapes=[pltpu.VMEM((1, LANES), F32)],
        compiler_params=_params(1), name="proj_ext" if ext else "proj_dec")(*args)


def _attn_prompt_kernel(qe_ref, ke_ref, vt_ref, o_ref, s0_ref, s1_ref, s2_ref, s3_ref, m_ref, acc_ref,
                        mx_ref, *, head_dim):
    bufs = (s0_ref, s1_ref, s2_ref, s3_ref)
    slot_of = {id(b): s for s, b in enumerate(bufs)}
    tq = o_ref.shape[0]
    ts = s0_ref.shape[1]
    hr = acc_ref.shape[1]
    i = pl.program_id(2)
    kv_i = lax.broadcasted_iota(jnp.int32, (ts, tq), 0)
    q_i = lax.broadcasted_iota(jnp.int32, (ts, tq), 1)
    m_ref[...] = jnp.full(m_ref.shape, -jnp.inf, F32)
    acc_ref[...] = jnp.zeros_like(acc_ref)

    def scores(qi, n, dst_ref):
        k0 = pl.multiple_of(n * ts, ts)
        q0 = pl.multiple_of(qi * tq, tq)
        for e in range(2):
            k = ke_ref[pl.ds(k0, ts), e * LANES:(e + 1) * LANES]
            q = qe_ref[pl.ds(q0, tq), e * LANES:(e + 1) * LANES]
            st = lax.dot_general(k, q, NT_DIMS, preferred_element_type=F32)
            dst_ref[e] = st
            mx_ref[slot_of[id(dst_ref)], e] = jnp.max(st, axis=0, keepdims=True)

    def update(n, src_ref, diag):
        for e in range(2):
            st = src_ref[e]
            if diag is not None:
                st = jnp.where(kv_i + diag * ts <= q_i, st, -jnp.inf)
                m_blk = jnp.max(st, axis=0, keepdims=True)
            else:
                m_blk = mx_ref[slot_of[id(src_ref)], e]
            m_prev = m_ref[e]
            m_new = jnp.maximum(m_prev, m_blk)
            pt = jnp.exp2(st - m_new).astype(BF16)
            alpha = jnp.exp2(m_prev - m_new)
            vt = vt_ref[0, n, e * hr:(e + 1) * hr, :]
            acc_ref[e] = alpha * acc_ref[e] + jnp.dot(vt, pt, preferred_element_type=F32)
            m_ref[e] = m_new

    n_buf = len(bufs)

    @pl.when(i == 0)
    def _():
        scores(0, 0, bufs[0])
        scores(0, 1, bufs[1])

    def steps(first, count):
        for u in range(count):
            scores(i, first + u + 2, bufs[(u + 2) % n_buf])
            update(first + u, bufs[u % n_buf], None)

    first = 0
    for width in (16, 8, 4, 2):
        trips = (2 * i - first) // width

        def body(j, carry, first=first, width=width):
            steps(first + width * j, width)
            return carry

        lax.fori_loop(0, trips, body, 0)
        first = first + width * trips
    nxt = jnp.minimum(i + 1, pl.num_programs(2) - 1)

    @pl.when(i % 2 == 0)
    def _():
        update(2 * i, bufs[0], 0)
        scores(nxt, 0, bufs[0])
        update(2 * i + 1, bufs[1], 1)
        scores(nxt, 1, bufs[1])

    @pl.when(i % 2 == 1)
    def _():
        scores(nxt, 0, bufs[0])
        update(2 * i, bufs[2], 0)
        scores(nxt, 1, bufs[1])
        update(2 * i + 1, bufs[3], 1)

    outs = []
    for e in range(2):
        acc = acc_ref[e]
        outs.append(acc[:head_dim, :] * (1.0 / acc[head_dim:head_dim + 1, :]))
    o_ref[...] = jnp.concatenate(outs, axis=0).T.astype(o_ref.dtype)


def _attn_prompt(qe, ke, vtb, *, batch, seq_len, n_heads, head_dim, tq):
    nsl, vrows, ts = vtb.shape[1:]
    hr = vrows // n_heads
    assert seq_len % tq == 0 and tq == 2 * ts and nsl * ts == seq_len
    nq = seq_len // tq
    return pl.pallas_call(
        functools.partial(_attn_prompt_kernel, head_dim=head_dim),
        grid=(batch, n_heads // 2, nq),
        in_specs=[pl.BlockSpec((seq_len, 2 * LANES), lambda b, p, i: (b, p)),
                  pl.BlockSpec((seq_len, 2 * LANES), lambda b, p, i: (b, p)),
                  pl.BlockSpec((1, nsl, 2 * hr, ts), lambda b, p, i: (b, 0, p, 0))],
        out_specs=pl.BlockSpec((tq, LANES), lambda b, p, i: (b * nq + i, p)),
        out_shape=jax.ShapeDtypeStruct((batch * seq_len, n_heads * head_dim), BF16),
        scratch_shapes=[pltpu.VMEM((2, ts, tq), F32)] * 4 + [pltpu.VMEM((2, 1, tq), F32),
                                                            pltpu.VMEM((2, hr, tq), F32),
                                                            pltpu.VMEM((4, 2, 1, tq), F32)],
        compiler_params=_params(3), name="attn_prompt")(qe, ke, vtb)


def _attn_paged_kernel(pt_ref, qs_ref, cnrow_ref, cnrep_ref, kn_ref, vn_ref, lt_ref, *rest,
                       pages, n_heads, head_dim):
    k_refs = rest[:pages]
    v_refs = rest[pages:2 * pages]
    lf_refs = rest[2 * pages:3 * pages]
    o_ref, qbd_ref, m_ref, l_ref, acc_ref, carry_ref = rest[3 * pages:]
    del pt_ref
    g = pl.program_id(1)
    t_new, d = kn_ref.shape[1], kn_ref.shape[2]
    rows = n_heads * t_new
    page = k_refs[0].shape[2]

    def online_update(s, v_bf16, v_transposed):
        m_prev = m_ref[...]
        m_new = jnp.maximum(m_prev, jnp.max(s, axis=1, keepdims=True))
        p = jnp.exp(s - m_new)
        alpha = jnp.exp(m_prev - m_new)
        l_ref[...] = alpha * l_ref[...] + jnp.sum(p, axis=1, keepdims=True)
        if v_transposed:
            pv = lax.dot_general(p.astype(BF16), v_bf16, NT_DIMS, preferred_element_type=F32)
        else:
            pv = jnp.dot(p.astype(BF16), v_bf16, preferred_element_type=F32)
        acc_ref[...] = alpha * acc_ref[...] + pv
        m_ref[...] = m_new

    @pl.when(g == 0)
    def _():
        qt = jnp.concatenate([qs_ref[0]] * n_heads, axis=0)
        r_i = lax.broadcasted_iota(jnp.int32, (rows, d), 0)
        l_i = lax.broadcasted_iota(jnp.int32, (rows, d), 1)
        qbd = jnp.where(r_i // t_new == l_i // head_dim, qt, 0.0).astype(BF16)
        qbd_ref[...] = qbd
        m_ref[...] = jnp.full(m_ref.shape, -jnp.inf, F32)
        l_ref[...] = jnp.zeros_like(l_ref)
        acc_ref[...] = jnp.zeros_like(acc_ref)
        carry_ref[...] = jnp.zeros_like(carry_ref)
        pad = jnp.zeros((LANES - t_new, d), F32)
        kn = jnp.concatenate([kn_ref[0], pad], axis=0).astype(BF16)
        vn = jnp.concatenate([vn_ref[0], pad], axis=0).astype(BF16)
        s = lax.dot_general(qbd, kn, NT_DIMS, preferred_element_type=F32)
        s = s + cnrow_ref[0] - cnrep_ref[0]
        r2 = lax.broadcasted_iota(jnp.int32, s.shape, 0)
        c2 = lax.broadcasted_iota(jnp.int32, s.shape, 1)
        s = jnp.where(c2 <= r2 % t_new, s, -jnp.inf)
        online_update(s, vn, False)

    lf_all = jnp.concatenate([lf_refs[p][0] for p in range(pages)], axis=0)
    hi, mid, lo = _split3(lf_all)
    within = jnp.dot(jnp.concatenate([hi, mid, lo], axis=0), lt_ref[...], preferred_element_type=F32)
    within = within[:pages * n_heads] + within[pages * n_heads:2 * pages * n_heads] + within[2 * pages * n_heads:]
    totals = jnp.sum(lf_all, axis=1, keepdims=True)
    run = carry_ref[...]
    sufs = [None] * pages
    for p in range(pages - 1, -1, -1):
        rows_p = slice(p * n_heads, (p + 1) * n_heads)
        sufs[p] = within[rows_p, :] + run
        run = run + totals[rows_p, :]
    carry_ref[...] = run
    suf = jnp.concatenate(sufs, axis=1)
    bias = jnp.concatenate([jnp.broadcast_to(suf[h:h + 1, :], (t_new, pages * page)) for h in range(n_heads)],
                           axis=0)
    half = max(pages // 2, 1)
    groups = [range(g0, min(g0 + half, pages)) for g0 in range(0, pages, half)]
    qbd = qbd_ref[...]
    s_parts = []
    for grp in groups:
        kt = jnp.concatenate([k_refs[p][0].astype(BF16) for p in grp], axis=1)
        cols = slice(grp[0] * page, (grp[-1] + 1) * page)
        s_parts.append(jnp.dot(qbd, kt, preferred_element_type=F32) + cnrow_ref[0] + bias[:, cols])
    for grp, s in zip(groups, s_parts):
        vt = jnp.concatenate([v_refs[p][0].astype(BF16) for p in grp], axis=1)
        online_update(s, vt, True)

    @pl.when(g == pl.num_programs(1) - 1)
    def _():
        full = acc_ref[...] / l_ref[...]
        out = jnp.zeros((t_new, d), F32)
        l_i = lax.broadcasted_iota(jnp.int32, (t_new, d), 1)
        for h in range(n_heads):
            out = jnp.where(l_i // head_dim == h, full[h * t_new:(h + 1) * t_new, :], out)
        o_ref[0] = out


def _attn_paged(page_table, qs, cn, k_new, v_new, cache_k, cache_v, cache_lf, *, n_heads, head_dim, pages):
    nseq, t_new, d = k_new.shape
    n_pool, page = cache_k.shape[0], cache_k.shape[1]
    n_pages = page_table.shape[1]
    rows = n_heads * t_new
    assert rows == LANES and page == LANES and n_pages % pages == 0
    steps = n_pages // pages
    cn_t = jnp.transpose(cn, (0, 2, 1))
    cnrow = cn_t.reshape(nseq, rows, 1)
    cnrep = jnp.repeat(cn_t, t_new, axis=1)
    cnrep = jnp.pad(cnrep, ((0, 0), (0, 0), (0, LANES - t_new)))
    lt = jnp.asarray(np.tril(np.ones((page, page), np.float32), -1), BF16)
    ck = jnp.transpose(cache_k, (0, 2, 3, 1)).reshape(n_pool, d, page)
    cv = jnp.transpose(cache_v, (0, 2, 3, 1)).reshape(n_pool, d, page)
    clf = jnp.transpose(cache_lf, (0, 2, 1))

    def seq_map(s, g, pt):
        return (s, 0, 0)

    def const_map(s, g, pt):
        return (0, 0)

    def page_map(i):
        def f(s, g, pt):
            return (pt[s, (steps - 1 - g) * pages + i], 0, 0)
        return f

    in_specs = [pl.BlockSpec((1, t_new, d), seq_map), pl.BlockSpec((1, rows, 1), seq_map),
                pl.BlockSpec((1, rows, LANES), seq_map), pl.BlockSpec((1, t_new, d), seq_map),
                pl.BlockSpec((1, t_new, d), seq_map), pl.BlockSpec((page, page), const_map)]
    in_specs += [pl.BlockSpec((1, d, page), page_map(i)) for i in range(pages)]
    in_specs += [pl.BlockSpec((1, d, page), page_map(i)) for i in range(pages)]
    in_specs += [pl.BlockSpec((1, n_heads, page), page_map(i)) for i in range(pages)]
    grid_spec = pltpu.PrefetchScalarGridSpec(
        num_scalar_prefetch=1, grid=(nseq, steps), in_specs=in_specs,
        out_specs=pl.BlockSpec((1, t_new, d), seq_map),
        scratch_shapes=[pltpu.VMEM((rows, d), BF16), pltpu.VMEM((rows, 1), F32), pltpu.VMEM((rows, 1), F32),
                        pltpu.VMEM((rows, d), F32), pltpu.VMEM((n_heads, 1), F32)])
    kern = functools.partial(_attn_paged_kernel, pages=pages, n_heads=n_heads, head_dim=head_dim)
    return pl.pallas_call(
        kern, grid_spec=grid_spec, out_shape=jax.ShapeDtypeStruct((nseq, t_new, d), F32),
        compiler_params=_params(2), name="attn_paged")(
            page_table, qs, cnrow, cnrep, k_new, v_new, lt,
            *([ck] * pages), *([cv] * pages), *([clf] * pages))


def _gated_kernel(*refs, kind, seq_len, final_norm, has_prev, has_pre, split):
    it = iter(refs)
    x_ref, g_ref = next(it), next(it)
    a_ref, wo_ref = (next(it), next(it)) if has_pre else (None, None)
    up_refs = [next(it) for _ in range(2 if kind == "ffn" else 3)]
    dw_ref, wd_ref = next(it), next(it)
    gfin_ref = next(it) if final_norm else None
    p0_ref, p1_ref = (next(it), next(it)) if has_prev else (None, None)
    o_ref, st_ref, h_ref, gate_ref, ext_ref = next(it), next(it), next(it), next(it), next(it)
    res_ref = next(it) if has_pre else x_ref
    acc_ref = next(it) if split else None
    tm = x_ref.shape[0]
    tf = up_refs[0].shape[1]
    r = pl.program_id(0)
    f = pl.program_id(1)
    pad = ext_ref.shape[1] - tm

    @pl.when(f == 0)
    def _():
        x_in = x_ref[...]
        if has_pre:
            x_in = x_in + jnp.dot(a_ref[...], wo_ref[...], preferred_element_type=F32)
            res_ref[...] = x_in
        h_ref[...] = _rmsnorm(x_in, g_ref[...]).astype(BF16)

    h = h_ref[...]
    ext = ext_ref.at[f]
    tiles_per_seq = 1 if has_prev else seq_len // tm

    @pl.when(r % tiles_per_seq == 0)
    def _():
        ext[0:pad, :] = jnp.zeros((pad, tf), F32)

    for c0 in range(0, tf, GATE_CHUNK):
        cols = slice(c0, min(c0 + GATE_CHUNK, tf))
        ups = [jnp.dot(h, w[:, cols], preferred_element_type=F32) for w in up_refs]
        u = ups[0] if kind == "ffn" else ups[1] * ups[2]
        ext[pad:pad + tm, cols] = u
        u1 = ext[pad - 1:pad - 1 + tm, cols]
        u2 = ext[pad - 2:pad - 2 + tm, cols]
        ext[0:pad, cols] = ext[tm:tm + pad, cols]
        if has_prev:
            t = lax.broadcasted_iota(jnp.int32, u.shape, 0) % seq_len
            p0, p1 = p0_ref[:, cols], p1_ref[:, cols]
            u1 = jnp.where(t == 0, p1, u1)
            u2 = jnp.where(t == 0, p0, jnp.where(t == 1, p1, u2))
            st_ref[:, cols] = u
        else:
            st_ref[0, :, cols] = u[tm - (CONV_W - 1):, :]
        conv = dw_ref[0:1, cols] * u2 + dw_ref[1:2, cols] * u1 + dw_ref[2:3, cols] * u
        if kind == "ffn":
            gated = conv * jax.nn.sigmoid(conv) * ups[1]
        else:
            gated = ups[0] * conv
        gate_ref[:, cols] = gated.astype(BF16)

    down = jnp.dot(gate_ref[...], wd_ref[...], preferred_element_type=F32)
    last = pl.num_programs(1) - 1
    if acc_ref is not None:
        @pl.when(f == 0)
        def _():
            acc_ref[...] = down

        @pl.when(jnp.logical_and(f > 0, f < last))
        def _():
            acc_ref[...] += down

    @pl.when(f == last)
    def _():
        y = res_ref[...] + down
        if acc_ref is not None:
            y = y + acc_ref[...]
        if final_norm:
            y = _rmsnorm(y, gfin_ref[...])
        o_ref[...] = y


def _gated(x2, g, ups, dw, wd, *, kind, seq_len, tm, tf, gfin=None, prev=None, pre=None):
    r, d = x2.shape
    fdim = ups[0].shape[1]
    assert r % tm == 0 and fdim % tf == 0
    has_prev = prev is not None
    if has_prev:
        assert tm == r and tm % seq_len == 0 and seq_len >= CONV_W - 1
    else:
        assert seq_len % tm == 0
    nf = fdim // tf
    n_up = len(ups)
    pad = 8
    row = lambda i, j: (i, 0)
    const = lambda i, j: (0, 0)
    once = dict(pipeline_mode=pl.Buffered(1)) if nf == 1 else {}
    in_specs = [pl.BlockSpec((tm, d), row), pl.BlockSpec((1, d), const)]
    args = [x2, g]
    if pre is not None:
        a, wo = pre
        in_specs += [pl.BlockSpec((tm, a.shape[1]), row),
                     pl.BlockSpec(wo.shape, const, pipeline_mode=pl.Buffered(1))]
        args += [a, wo]
    in_specs += [pl.BlockSpec((d, tf), lambda i, j: (0, j), **once) for _ in range(n_up)]
    in_specs += [pl.BlockSpec((CONV_W, tf), lambda i, j: (0, j)),
                 pl.BlockSpec((tf, d), lambda i, j: (j, 0), **once)]
    args += [*ups, dw, wd]
    if gfin is not None:
        in_specs.append(pl.BlockSpec((1, d), const))
        args.append(gfin)
    if has_prev:
        in_specs += [pl.BlockSpec((tm, tf), lambda i, j: (i, j))] * 2
        args += list(prev)
        st_shape = jax.ShapeDtypeStruct((r, fdim), F32)
        st_spec = pl.BlockSpec((tm, tf), lambda i, j: (i, j))
    else:
        st_shape = jax.ShapeDtypeStruct((r // tm, CONV_W - 1, fdim), F32)
        st_spec = pl.BlockSpec((1, CONV_W - 1, tf), lambda i, j: (i, 0, j))
    kern = functools.partial(_gated_kernel, kind=kind, seq_len=seq_len, final_norm=gfin is not None,
                             has_prev=has_prev, has_pre=pre is not None, split=nf > 1)
    scratch = [pltpu.VMEM((tm, d), BF16), pltpu.VMEM((tm, tf), BF16), pltpu.VMEM((nf, tm + pad, tf), F32)]
    scratch += [pltpu.VMEM((tm, d), F32)] * ((pre is not None) + (nf > 1))
    return pl.pallas_call(
        kern, grid=(r // tm, nf), in_specs=in_specs,
        out_specs=[pl.BlockSpec((tm, d), row), st_spec],
        out_shape=[jax.ShapeDtypeStruct((r, d), F32), st_shape],
        scratch_shapes=scratch,
        compiler_params=_params(2), name="gated_" + kind + ("_dec" if has_prev else ""))(*args)


def _tiles(seq_len, d_ff, d_model):
    def largest_divisor(n, cap, unit):
        best = unit
        for t in range(unit, min(n, cap) + 1, unit):
            if n % t == 0:
                best = t
        return best
    return dict(proj=largest_divisor(seq_len, 256, 8), attn=largest_divisor(seq_len, 512, 2 * LANES),
                rows=largest_divisor(seq_len, 512, 8),
                ff=d_ff, conv=d_model)


def _trunk(x3, *, paged, conv_prev, ffn_prev, w, cache, n_heads):
    b, t, d = x3.shape
    head_dim = d // n_heads
    r = b * t
    x2 = x3.reshape(r, d)
    d_ff = w["ffn_a"][0].shape[1]
    tl = _tiles(t, d_ff, d)
    if paged:
        tl = dict(tl, proj=r, rows=r)
    depth = len(w["ffn_a"])
    ks, vs, lfs, convs, ffns = [], [], [], [], []
    y = None
    for layer in range(depth):
        gm = w["norm_mix"][layer][None, :]
        if layer % 2 == 0:
            a = layer // 2
            if paged:
                k, v, lf, cn, qs = _proj(x2, gm, w["qkv"][a], w["wf"][a], w["bf"][a], seq_len=t,
                                         n_heads=n_heads, head_dim=head_dim, tm=tl["proj"])
                o = _attn_paged(cache["page_table"], qs.reshape(b, t, d), cn[:, :n_heads].reshape(b, t, n_heads),
                                k.reshape(b, t, d), v.reshape(b, t, d), cache["k"][a], cache["v"][a],
                                cache["lf"][a], n_heads=n_heads, head_dim=head_dim, pages=cache["pages"])
                o = o.reshape(r, d).astype(BF16)
                ks.append(k.reshape(b, t, n_heads, head_dim))
                vs.append(v.reshape(b, t, n_heads, head_dim))
            else:
                kt, vt, lf, qe, ke, vtb = _proj(x2, gm, w["qkv"][a], w["wf"][a], w["bf"][a], seq_len=t,
                                                n_heads=n_heads, head_dim=head_dim, tm=tl["proj"],
                                                tk=tl["attn"] // 2)
                o = _attn_prompt(qe, ke, vtb, batch=b, seq_len=t, n_heads=n_heads, head_dim=head_dim,
                                 tq=tl["attn"])
                ks.append(jnp.transpose(kt.reshape(b, n_heads, head_dim, t), (0, 3, 1, 2)))
                vs.append(jnp.transpose(vt.reshape(b, n_heads, head_dim, t), (0, 3, 1, 2)))
            pre = (o, w["wo"][a])
            lfs.append(lf[:, :n_heads].reshape(b, t, n_heads))
        else:
            pre = None
            c = layer // 2
            ups = [w["conv_in"][c][:, j * d:(j + 1) * d] for j in range(3)]
            prev = None
            if paged:
                prev = tuple(jnp.repeat(conv_prev[c][:, j], t, axis=0) for j in range(CONV_W - 1))
            x2, st = _gated(x2, gm, ups, w["conv_dw"][c], w["conv_out"][c], kind="conv", seq_len=t,
                            tm=tl["rows"], tf=tl["conv"], prev=prev)
            convs.append(_state_rows(st, b, t, paged, tl["rows"]))
        gf = w["norm_ffn"][layer][None, :]
        prev = None
        if paged:
            prev = tuple(jnp.repeat(ffn_prev[layer][:, j], t, axis=0) for j in range(CONV_W - 1))
        gfin = w["norm_final"][None, :] if layer == depth - 1 else None
        x2, st = _gated(x2, gf, [w["ffn_a"][layer], w["ffn_b"][layer]], w["ffn_dw"][layer],
                        w["ffn_down"][layer], kind="ffn", seq_len=t, tm=tl["rows"], tf=tl["ff"],
                        gfin=gfin, prev=prev, pre=pre)
        ffns.append(_state_rows(st, b, t, paged, tl["rows"]))
        y = x2
    return (y.reshape(b, t, d), jnp.stack(ks), jnp.stack(vs), jnp.stack(lfs), jnp.stack(convs),
            jnp.stack(ffns))


def _state_rows(st, b, t, paged, tm):
    if paged:
        return st.reshape(b, t, -1)[:, t - (CONV_W - 1):]
    tiles_per_seq = t // tm
    return st[tiles_per_seq - 1::tiles_per_seq]


def kernel(x_prompt, x_sample, cache_k, cache_v, cache_logf, state_conv, state_ffn, page_table, norm_mix, norm_ffn, norm_final, attn_w_qkv, attn_w_f, attn_b_f, attn_w_o, conv_w_in, conv_w_dw, conv_w_out, ffn_w_a, ffn_w_b, ffn_w_dw, ffn_w_down):
    n_attn, d, n_heads = attn_w_f.shape
    n_conv = conv_w_in.shape[0]
    depth = ffn_w_a.shape[0]
    wf = jnp.pad(attn_w_f, ((0, 0), (0, 0), (0, LANES - n_heads))).astype(BF16)
    bf = jnp.pad(attn_b_f, ((0, 0), (0, LANES - n_heads)))[:, None, :]
    w = dict(
        norm_mix=norm_mix, norm_ffn=norm_ffn, norm_final=norm_final,
        qkv=[attn_w_qkv[a].astype(BF16) for a in range(n_attn)],
        wf=[wf[a] for a in range(n_attn)], bf=[bf[a] for a in range(n_attn)],
        wo=[attn_w_o[a].astype(BF16) for a in range(n_attn)],
        conv_in=[conv_w_in[c].astype(BF16) for c in range(n_conv)],
        conv_dw=[conv_w_dw[c] for c in range(n_conv)],
        conv_out=[conv_w_out[c].astype(BF16) for c in range(n_conv)],
        ffn_a=[ffn_w_a[l].astype(BF16) for l in range(depth)],
        ffn_b=[ffn_w_b[l].astype(BF16) for l in range(depth)],
        ffn_dw=[ffn_w_dw[l] for l in range(depth)],
        ffn_down=[ffn_w_down[l].astype(BF16) for l in range(depth)])
    n_pages = page_table.shape[1]
    pages = next(p for p in (16, 8, 4, 2, 1) if n_pages % p == 0)
    cache = dict(k=cache_k, v=cache_v, lf=cache_logf, page_table=page_table, pages=pages)
    prompt = _trunk(x_prompt, paged=False, conv_prev=None, ffn_prev=None, w=w, cache=None, n_heads=n_heads)
    sample = _trunk(x_sample, paged=True, conv_prev=state_conv, ffn_prev=state_ffn, w=w, cache=cache,
                    n_heads=n_heads)
    y_p, k_p, v_p, lf_p, conv_p, ffn_p = prompt
    y_s, k_s, v_s, lf_s, conv_s, ffn_s = sample
    return (y_p, y_s, k_p, v_p, lf_p, conv_p, ffn_p, k_s, v_s, lf_s, conv_s, ffn_s)
```
